```python
import math
import jax
import jax.numpy as jnp
from jax import lax
import numpy as np

D_MODEL = 2048
BATCH = 4
SEQ = 8192
DEPTH = 4
DEC_BATCH = 16
DEC_SEQ = 16
PAST_LEN = 2048

CHUNK = 64
N_PAIRS = DEPTH // 2
CONV_W = 4
H_A = 8
DK_A = 128
DV_A = D_MODEL // (2 * H_A)
QKV_A = H_A * (2 * DK_A + DV_A)
D_INNER_B = D_MODEL // 2
P_B = 64
H_B = D_INNER_B // P_B
G_B = 2
R_B = H_B // G_B
N_B = 128
XBC_B = D_INNER_B + 2 * G_B * N_B
IN_AB = QKV_A + 2 * H_A + H_A * DV_A + D_INNER_B + XBC_B + H_B
D_MIX_AB = H_A * DV_A + D_INNER_B
H_C = 16
EXP_C = 128
HD_C = D_MODEL // H_C
FORGET_C = H_C * EXP_C
D_MIX_C = H_C * HD_C
IN_C = 2 * FORGET_C + 2 * D_MIX_C
N_EXPERTS = 32
TOP_K = 4
D_FF = D_MODEL
SWIGLU_LIMIT = 7.0
SWIGLU_ALPHA = 1.702
MOE_BLOCK = 128
ALPHA_DN = (2 * DEPTH) ** 0.25
BETA_DN = (8 * DEPTH) ** -0.25
LN_EPS = 1e-5
RMS_EPS = 1e-6

kernel_name = 'hybrid_gdn_ssd_hgrn2_moe_stream_step'


def split_cols(t, sizes):
    idx = [int(i) for i in np.cumsum(sizes)[:-1]]
    return jnp.split(t, idx, axis=-1)


def layer_norm(x, g, b):
    xf = x.astype(jnp.float32)
    mu = jnp.mean(xf, axis=-1, keepdims=True)
    var = jnp.mean(jnp.square(xf - mu), axis=-1, keepdims=True)
    y = (xf - mu) * lax.rsqrt(var + LN_EPS) * g.astype(jnp.float32) + b.astype(jnp.float32)
    return y.astype(x.dtype)


def l2norm(t):
    return t * lax.rsqrt(jnp.sum(t * t, axis=-1, keepdims=True) + 1e-6)


def gated_rms_norm(o, z, w):
    o = o * lax.rsqrt(jnp.mean(o * o, axis=-1, keepdims=True) + RMS_EPS)
    return o * w.astype(jnp.float32) * jax.nn.silu(z)


def causal_conv(u, prev, w):
    L = u.shape[1]
    ucat = jnp.concatenate([prev.astype(u.dtype), u], axis=1)
    out = sum(ucat[:, j:j + L] * w[j].astype(u.dtype) for j in range(CONV_W))
    return out, ucat[:, L:]


def to_chunks(t, blk):
    b, l = t.shape[:2]
    t = t.reshape((b, l // blk, blk) + t.shape[2:])
    return jnp.moveaxis(jnp.moveaxis(t, 1, 0), 2, 3)


def from_chunks(o):
    o = jnp.moveaxis(jnp.moveaxis(o, 3, 2), 0, 1)
    return o.reshape((o.shape[0], o.shape[1] * o.shape[2]) + o.shape[3:])


def gdn_scan(q, k, v, log_a, beta, s0):
    blk = min(CHUNK, q.shape[1])
    dv = v.shape[-1]
    incl = jnp.tril(jnp.ones((blk, blk), bool))
    strict = jnp.tril(jnp.ones((blk, blk), bool), -1)

    def step(s, inp):
        qc, kc, vc, ac, bc = inp
        g = jnp.cumsum(ac, axis=-1)
        gamma = jnp.exp(jnp.where(incl, g[..., :, None] - g[..., None, :], -jnp.inf))
        a_mat = jnp.where(strict, bc[..., :, None] * jnp.einsum('bhtd,bhsd->bhts', kc, kc) * gamma, 0.0)
        rhs = jnp.concatenate([vc * bc[..., None], kc * (bc * jnp.exp(g))[..., None]], axis=-1)
        sol = lax.linalg.triangular_solve(a_mat, rhs, left_side=True, lower=True, unit_diagonal=True)
        u, w = sol[..., :dv], sol[..., dv:]
        delta = u - jnp.einsum('bhtd,bhde->bhte', w, s)
        o = (jnp.einsum('bhtd,bhde->bhte', qc * jnp.exp(g)[..., None], s)
             + jnp.einsum('bhts,bhse->bhte', jnp.einsum('bhtd,bhsd->bhts', qc, kc) * gamma, delta))
        g_end = g[..., -1:]
        s = (jnp.exp(g_end)[..., None] * s
             + jnp.einsum('bhsd,bhse->bhde', kc * jnp.exp(g_end - g)[..., None], delta))
        return s, o

    s, o = lax.scan(step, s0, tuple(to_chunks(t, blk) for t in (q, k, v, log_a, beta)))
    return from_chunks(o), s


def ssd_scan(xdt, a, bm, cm, s0):
    blk = min(CHUNK, xdt.shape[1])
    incl = jnp.tril(jnp.ones((blk, blk), bool))

    def step(s, inp):
        xc, ac, bc, cc = inp
        cum = jnp.cumsum(ac, axis=2)
        seg = jnp.exp(jnp.where(incl[:, :, None], cum[:, :, :, None, :] - cum[:, :, None, :, :], -jnp.inf))
        cb = jnp.einsum('bgtn,bgsn->bgts', cc, bc)
        y = (jnp.einsum('bgts,bgtsr,bgsrp->bgtrp', cb, seg, xc)
             + jnp.einsum('bgtn,bgrnp->bgtrp', cc, s) * jnp.exp(cum)[..., None])
        c_end = cum[:, :, -1:]
        s = (jnp.exp(c_end[:, :, 0])[..., None, None] * s
             + jnp.einsum('bgsn,bgsrp->bgrnp', bc, xc * jnp.exp(c_end - cum)[..., None]))
        return s, y

    s, y = lax.scan(step, s0, tuple(to_chunks(t, blk) for t in (xdt, a, bm, cm)))
    return from_chunks(y), s


def gla_scan(q, k, v, log_f, s0):
    blk = min(CHUNK, q.shape[1])
    incl = jnp.tril(jnp.ones((blk, blk), bool))

    def step(s, inp):
        qc, kc, vc, fc = inp
        b = jnp.cumsum(fc, axis=2)
        dec = jnp.exp(jnp.where(incl[:, :, None], b[:, :, :, None, :] - b[:, :, None, :, :], -jnp.inf))
        attn = jnp.einsum('bhtd,bhsd,bhtsd->bhts', qc, kc, dec)
        o = jnp.einsum('bhtd,bhde->bhte', qc * jnp.exp(b), s) + jnp.einsum('bhts,bhse->bhte', attn, vc)
        b_end = b[:, :, -1:]
        s = (jnp.exp(b_end[:, :, 0])[..., None] * s
             + jnp.einsum('bhsd,bhse->bhde', kc * jnp.exp(b_end - b), vc))
        return s, o

    s, o = lax.scan(step, s0, tuple(to_chunks(t, blk) for t in (q, k, v, log_f)))
    return from_chunks(o), s


def mixer_ab(x, conv_g, s_g, conv_s, s_s, w_in, conv_gdn_w, gdn_a_log, gdn_dt_bias, gdn_norm_w,
             conv_ssd_w, conv_ssd_b, ssd_a_log, ssd_dt_bias, ssd_d, ssd_norm_w, w_out):
    f32 = jnp.float32
    bsz, L, _ = x.shape
    qkv, b_raw, a_raw, z_a, z_b, xbc, dt_raw = split_cols(
        (x @ w_in).astype(f32), [QKV_A, H_A, H_A, H_A * DV_A, D_INNER_B, XBC_B, H_B])
    qkv, conv_g_new = causal_conv(qkv, conv_g, conv_gdn_w)
    q, k, v = split_cols(jax.nn.silu(qkv), [H_A * DK_A, H_A * DK_A, H_A * DV_A])
    q = l2norm(q.reshape(bsz, L, H_A, DK_A)) * DK_A ** -0.5
    k = l2norm(k.reshape(bsz, L, H_A, DK_A))
    v = v.reshape(bsz, L, H_A, DV_A)
    beta = jax.nn.sigmoid(b_raw)
    log_a = -jnp.exp(gdn_a_log.astype(f32)) * jax.nn.softplus(a_raw + gdn_dt_bias.astype(f32))
    o_a, s_g_new = gdn_scan(q, k, v, log_a, beta, s_g.astype(f32))
    o_a = gated_rms_norm(o_a, z_a.reshape(bsz, L, H_A, DV_A), gdn_norm_w).reshape(bsz, L, H_A * DV_A)
    xbc, conv_s_new = causal_conv(xbc, conv_s, conv_ssd_w)
    xs, bm, cm = split_cols(jax.nn.silu(xbc + conv_ssd_b.astype(f32)), [D_INNER_B, G_B * N_B, G_B * N_B])
    dt = jax.nn.softplus(dt_raw + ssd_dt_bias.astype(f32)).reshape(bsz, L, G_B, R_B)
    xh = xs.reshape(bsz, L, G_B, R_B, P_B)
    a = -jnp.exp(ssd_a_log.astype(f32)).reshape(G_B, R_B) * dt
    y, s_s_new = ssd_scan(xh * dt[..., None], a, bm.reshape(bsz, L, G_B, N_B), cm.reshape(bsz, L, G_B, N_B),
                          s_s.astype(f32).reshape(bsz, G_B, R_B, N_B, P_B))
    y = y + ssd_d.astype(f32).reshape(G_B, R_B, 1) * xh
    y = (y.reshape(bsz, L, D_INNER_B) * jax.nn.silu(z_b)).reshape(bsz, L, G_B, D_INNER_B // G_B)
    y = (y * lax.rsqrt(jnp.mean(y * y, axis=-1, keepdims=True) + RMS_EPS)).reshape(bsz, L, D_INNER_B)
    y = y * ssd_norm_w.astype(f32)
    out = jnp.concatenate([o_a, y], axis=-1).astype(x.dtype) @ w_out
    return out, conv_g_new, s_g_new, conv_s_new, s_s_new.reshape(bsz, H_B, N_B, P_B)


def mixer_c(x, lb, s_prev, w_in, norm_w, w_out):
    f32 = jnp.float32
    bsz, L, _ = x.shape
    q, f, i, g = split_cols((x @ w_in).astype(f32), [FORGET_C, FORGET_C, D_MIX_C, D_MIX_C])
    sig_f = jax.nn.sigmoid(f)
    log_f = jnp.log(lb + (1.0 - lb) * sig_f)
    k = (1.0 - lb) * (1.0 - sig_f)
    o, s_new = gla_scan(jax.nn.silu(q).reshape(bsz, L, H_C, EXP_C) * EXP_C ** -0.5,
                        k.reshape(bsz, L, H_C, EXP_C), i.reshape(bsz, L, H_C, HD_C),
                        log_f.reshape(bsz, L, H_C, EXP_C), s_prev.astype(f32))
    o = gated_rms_norm(o, g.reshape(bsz, L, H_C, HD_C), norm_w).reshape(bsz, L, D_MIX_C)
    return o.astype(x.dtype) @ w_out, s_new


def moe(x, router_w, router_b, w_gate_up, b_gate_up, w_down, b_down):
    shp = x.shape
    xt = x.reshape(-1, shp[-1])
    T = xt.shape[0]
    logits = (xt @ router_w + router_b).astype(jnp.float32)
    top_val, top_idx = lax.top_k(logits, TOP_K)
    gates = jax.nn.softmax(top_val, axis=-1)
    m = T * TOP_K
    flat_e = top_idx.reshape(m)
    order = jnp.argsort(flat_e, stable=True)
    e_sorted = flat_e[order]
    tok_sorted = order // TOP_K
    gate_sorted = gates.reshape(m)[order]
    counts = jnp.bincount(flat_e, length=N_EXPERTS)
    padded = (counts + MOE_BLOCK - 1) // MOE_BLOCK * MOE_BLOCK
    pad_end = jnp.cumsum(padded)
    pad_start = pad_end - padded
    start = jnp.cumsum(counts) - counts
    dest = pad_start[e_sorted] + jnp.arange(m) - start[e_sorted]
    n_blocks = -(-m // MOE_BLOCK) + N_EXPERTS
    slot_tok = jnp.zeros((n_blocks * MOE_BLOCK,), jnp.int32).at[dest].set(tok_sorted)
    block_expert = jnp.minimum(jnp.searchsorted(pad_end, jnp.arange(n_blocks) * MOE_BLOCK, side='right'),
                               N_EXPERTS - 1)

    def expert_block(args):
        xb, e = args
        h = xb @ w_gate_up[e] + b_gate_up[e]
        glu = jnp.minimum(h[:, :D_FF], SWIGLU_LIMIT)
        lin = jnp.clip(h[:, D_FF:], -SWIGLU_LIMIT, SWIGLU_LIMIT)
        act = glu * jax.nn.sigmoid(SWIGLU_ALPHA * glu) * (lin + 1.0)
        return act @ w_down[e] + b_down[e]

    y_blocks = lax.map(expert_block, (xt[slot_tok].reshape(n_blocks, MOE_BLOCK, -1), block_expert))
    y_assign = y_blocks.reshape(n_blocks * MOE_BLOCK, -1)[dest] * gate_sorted[:, None].astype(x.dtype)
    return jax.ops.segment_sum(y_assign, tok_sorted, num_segments=T).reshape(shp)


def trunk(x, conv_gdn, st_gdn, conv_ssd, st_ssd, st_hgrn, p):
    lb_w = jax.nn.softmax(p['hgrn_lower_bounds'].astype(jnp.float32), axis=0)
    lower_bounds = jnp.cumsum(lb_w, axis=0) - lb_w[0]
    new_cg, new_sg, new_cs, new_ss, new_sh = [], [], [], [], []
    for layer in range(DEPTH):
        j = layer // 2
        if layer % 2 == 0:
            h, cg, sg, cs, ss = mixer_ab(
                x, conv_gdn[j], st_gdn[j], conv_ssd[j], st_ssd[j], p['w_in_ab'][j], p['conv_gdn_w'][j],
                p['gdn_a_log'][j], p['gdn_dt_bias'][j], p['gdn_norm_w'][j], p['conv_ssd_w'][j],
                p['conv_ssd_b'][j], p['ssd_a_log'][j], p['ssd_dt_bias'][j], p['ssd_d'][j],
                p['ssd_norm_w'][j], p['w_out_ab'][j])
            new_cg.append(cg)
            new_sg.append(sg)
            new_cs.append(cs)
            new_ss.append(ss)
        else:
            h, sh = mixer_c(x, lower_bounds[j], st_hgrn[j], p['w_in_c'][j], p['hgrn_norm_w'][j], p['w_out_c'][j])
            new_sh.append(sh)
        x = layer_norm(ALPHA_DN * x + h, p['ln_mix_g'][layer], p['ln_mix_b'][layer])
        f = moe(x, p['router_w'][layer], p['router_b'][layer], p['expert_w_gate_up'][layer],
                p['expert_b_gate_up'][layer], p['expert_w_down'][layer], p['expert_b_down'][layer])
        x = layer_norm(ALPHA_DN * x + f, p['ln_ffn_g'][layer], p['ln_ffn_b'][layer])
    return x, jnp.stack(new_cg), jnp.stack(new_sg), jnp.stack(new_cs), jnp.stack(new_ss), jnp.stack(new_sh)


def setup_inputs(seed: int = 0) -> dict:
    key = jax.random.key(seed)
    ks = jax.random.split(key, 33)
    f32 = jnp.float32

    def nrm(i, shape, scale):
        return scale * jax.random.normal(ks[i], shape, f32)

    def dt_bias(i, shape):
        dt = jnp.exp(jax.random.uniform(ks[i], shape, f32, math.log(1e-3), math.log(1e-1)))
        return dt + jnp.log(-jnp.expm1(-dt))

    def a_log(i, shape):
        return jnp.log(jax.random.uniform(ks[i], shape, f32, 1.0, 16.0))

    return {
        'x_prompt': nrm(0, (BATCH, SEQ, D_MODEL), 1.0),
        'x_sample': nrm(1, (DEC_BATCH, DEC_SEQ, D_MODEL), 1.0),
        'cache_conv_gdn': nrm(2, (N_PAIRS, DEC_BATCH, CONV_W - 1, QKV_A), 1.0),
        'state_gdn': nrm(3, (N_PAIRS, DEC_BATCH, H_A, DK_A, DV_A), 0.1),
        'cache_conv_ssd': nrm(4, (N_PAIRS, DEC_BATCH, CONV_W - 1, XBC_B), 1.0),
        'state_ssd': nrm(5, (N_PAIRS, DEC_BATCH, H_B, N_B, P_B), 0.1),
        'state_hgrn': nrm(6, (N_PAIRS, DEC_BATCH, H_C, EXP_C, HD_C), 0.1),
        'w_in_ab': nrm(7, (N_PAIRS, D_MODEL, IN_AB), D_MODEL ** -0.5),
        'conv_gdn_w': nrm(8, (N_PAIRS, CONV_W, QKV_A), CONV_W ** -0.5),
        'gdn_a_log': a_log(9, (N_PAIRS, H_A)),
        'gdn_dt_bias': dt_bias(10, (N_PAIRS, H_A)),
        'gdn_norm_w': 1.0 + nrm(11, (N_PAIRS, DV_A), 0.01),
        'conv_ssd_w': nrm(12, (N_PAIRS, CONV_W, XBC_B), CONV_W ** -0.5),
        'conv_ssd_b': nrm(13, (N_PAIRS, XBC_B), 0.01),
        'ssd_a_log': a_log(14, (N_PAIRS, H_B)),
        'ssd_dt_bias': dt_bias(15, (N_PAIRS, H_B)),
        'ssd_d': 1.0 + nrm(16, (N_PAIRS, H_B), 0.1),
        'ssd_norm_w': 1.0 + nrm(17, (N_PAIRS, D_INNER_B), 0.01),
        'w_out_ab': nrm(18, (N_PAIRS, D_MIX_AB, D_MODEL), BETA_DN * D_MIX_AB ** -0.5),
        'w_in_c': nrm(19, (N_PAIRS, D_MODEL, IN_C), D_MODEL ** -0.5),
        'hgrn_lower_bounds': nrm(20, (N_PAIRS, FORGET_C), 0.1),
        'hgrn_norm_w': 1.0 + nrm(21, (N_PAIRS, HD_C), 0.01),
        'w_out_c': nrm(22, (N_PAIRS, D_MIX_C, D_MODEL), BETA_DN * D_MIX_C ** -0.5),
        'router_w': nrm(23, (DEPTH, D_MODEL, N_EXPERTS), D_MODEL ** -0.5),
        'router_b': nrm(24, (DEPTH, N_EXPERTS), 0.01),
        'expert_w_gate_up': nrm(25, (DEPTH, N_EXPERTS, D_MODEL, 2 * D_FF), D_MODEL ** -0.5),
        'expert_b_gate_up': nrm(26, (DEPTH, N_EXPERTS, 2 * D_FF), 0.01),
        'expert_w_down': nrm(27, (DEPTH, N_EXPERTS, D_FF, D_MODEL), BETA_DN * D_FF ** -0.5),
        'expert_b_down': nrm(28, (DEPTH, N_EXPERTS, D_MODEL), 0.01),
        'ln_mix_g': 1.0 + nrm(29, (DEPTH, D_MODEL), 0.01),
        'ln_mix_b': nrm(30, (DEPTH, D_MODEL), 0.01),
        'ln_ffn_g': 1.0 + nrm(31, (DEPTH, D_MODEL), 0.01),
        'ln_ffn_b': nrm(32, (DEPTH, D_MODEL), 0.01),
    }


def reference(x_prompt, x_sample, cache_conv_gdn, state_gdn, cache_conv_ssd, state_ssd, state_hgrn,
              w_in_ab, conv_gdn_w, gdn_a_log, gdn_dt_bias, gdn_norm_w, conv_ssd_w, conv_ssd_b,
              ssd_a_log, ssd_dt_bias, ssd_d, ssd_norm_w, w_out_ab, w_in_c, hgrn_lower_bounds,
              hgrn_norm_w, w_out_c, router_w, router_b, expert_w_gate_up, expert_b_gate_up,
              expert_w_down, expert_b_down, ln_mix_g, ln_mix_b, ln_ffn_g, ln_ffn_b):
    p = dict(w_in_ab=w_in_ab, conv_gdn_w=conv_gdn_w, gdn_a_log=gdn_a_log, gdn_dt_bias=gdn_dt_bias,
             gdn_norm_w=gdn_norm_w, conv_ssd_w=conv_ssd_w, conv_ssd_b=conv_ssd_b, ssd_a_log=ssd_a_log,
             ssd_dt_bias=ssd_dt_bias, ssd_d=ssd_d, ssd_norm_w=ssd_norm_w, w_out_ab=w_out_ab, w_in_c=w_in_c,
             hgrn_lower_bounds=hgrn_lower_bounds, hgrn_norm_w=hgrn_norm_w, w_out_c=w_out_c,
             router_w=router_w, router_b=router_b, expert_w_gate_up=expert_w_gate_up,
             expert_b_gate_up=expert_b_gate_up, expert_w_down=expert_w_down, expert_b_down=expert_b_down,
             ln_mix_g=ln_mix_g, ln_mix_b=ln_mix_b, ln_ffn_g=ln_ffn_g, ln_ffn_b=ln_ffn_b)
    f32 = jnp.float32
    bp = x_prompt.shape[0]
    y_prompt, cg_p, sg_p, cs_p, ss_p, sh_p = trunk(
        x_prompt,
        jnp.zeros((N_PAIRS, bp, CONV_W - 1, QKV_A), f32),
        jnp.zeros((N_PAIRS, bp, H_A, DK_A, DV_A), f32),
        jnp.zeros((N_PAIRS, bp, CONV_W - 1, XBC_B), f32),
        jnp.zeros((N_PAIRS, bp, H_B, N_B, P_B), f32),
        jnp.zeros((N_PAIRS, bp, H_C, EXP_C, HD_C), f32),
        p)
    y_sample, cg_s, sg_s, cs_s, ss_s, sh_s = trunk(
        x_sample, cache_conv_gdn, state_gdn, cache_conv_ssd, state_ssd, state_hgrn, p)
    return (y_prompt, y_sample, cg_p, sg_p, cs_p, ss_p, sh_p, cg_s, sg_s, cs_s, ss_s, sh_s)
```

```python
import functools

import jax
import jax.numpy as jnp
import numpy as np
from jax import lax
from jax.experimental import pallas as pl
from jax.experimental.pallas import tpu as pltpu

F32 = jnp.float32
BF16 = jnp.bfloat16

CHUNK = 64
CONV_W = 4
G_B = 2
TOP_K = 4
SWIGLU_LIMIT = 7.0
SWIGLU_ALPHA = 1.702
LN_EPS = 1e-5
RMS_EPS = 1e-6

V7X_LANES = 128
V7X_MXU_DIM = 256
V7X_VMEM_LIMIT_BYTES = 56 * 1024 * 1024


def _pick(n, candidates):
    for c in candidates:
        if n % c == 0:
            return c
    return n


def _params(*sem):
    return pltpu.CompilerParams(dimension_semantics=sem, vmem_limit_bytes=V7X_VMEM_LIMIT_BYTES)


def _matmul_body(x_ref, w_ref, o_ref):
    o_ref[...] = jnp.dot(x_ref[...], w_ref[...], preferred_element_type=F32)


def _matmul(x, w):
    m, k = x.shape
    n = w.shape[1]
    tm = _pick(m, (768, 512, 256, 128, 64, 32, 16, 8))
    tn = _pick(n, (512, 256, 128))
    return pl.pallas_call(
        _matmul_body,
        grid=(m // tm, n // tn),
        in_specs=[pl.BlockSpec((tm, k), lambda i, j: (i, 0)),
                  pl.BlockSpec((k, tn), lambda i, j: (0, j))],
        out_specs=pl.BlockSpec((tm, tn), lambda i, j: (i, j)),
        out_shape=jax.ShapeDtypeStruct((m, n), F32),
        compiler_params=_params("parallel", "arbitrary"),
        name="dense_proj",
    )(x, w)


def _layer_norm_rows(y, g, b):
    mu = jnp.mean(y, axis=-1, keepdims=True)
    yc = y - mu
    var = jnp.mean(yc * yc, axis=-1, keepdims=True)
    return yc * lax.rsqrt(var + LN_EPS) * g + b


def _outproj_ln_body(mix_ref, w_ref, xres_ref, g_ref, b_ref, of_ref, ob_ref, *, alpha):
    h = jnp.dot(mix_ref[...], w_ref[...], preferred_element_type=F32)
    y = _layer_norm_rows(alpha * xres_ref[...] + h, g_ref[...], b_ref[...])
    of_ref[...] = y
    ob_ref[...] = y.astype(BF16)


def _outproj_ln(mix, w, xres, g, b, alpha):
    m, k = mix.shape
    d = w.shape[1]
    tm = _pick(m, (384, 256, 128, 64, 32, 16, 8))
    row = lambda i: (i, 0)
    fixed = lambda i: (0, 0)
    return pl.pallas_call(
        functools.partial(_outproj_ln_body, alpha=alpha),
        grid=(m // tm,),
        in_specs=[pl.BlockSpec((tm, k), row),
                  pl.BlockSpec((k, d), fixed),
                  pl.BlockSpec((tm, d), row),
                  pl.BlockSpec((1, d), fixed),
                  pl.BlockSpec((1, d), fixed)],
        out_specs=[pl.BlockSpec((tm, d), row), pl.BlockSpec((tm, d), row)],
        out_shape=[jax.ShapeDtypeStruct((m, d), F32), jax.ShapeDtypeStruct((m, d), BF16)],
        compiler_params=_params("parallel"),
        name="outproj_ln",
    )(mix, w, xres, g.reshape(1, d), b.reshape(1, d))


def _router_body(x_ref, wt_ref, b_ref, idx_ref, gate_ref):
    logits = lax.dot_general(wt_ref[...], x_ref[...], (((1,), (1,)), ((), ())),
                             precision=lax.Precision.HIGHEST, preferred_element_type=F32)
    logits = logits + b_ref[...]
    n_e = logits.shape[0]
    iota = lax.broadcasted_iota(jnp.int32, logits.shape, 0)
    vals, idxs = [], []
    for _ in range(TOP_K):
        m = jnp.max(logits, axis=0, keepdims=True)
        sel = jnp.min(jnp.where(logits == m, iota, n_e), axis=0, keepdims=True)
        vals.append(m)
        idxs.append(sel)
        logits = jnp.where(iota == sel, -jnp.inf, logits)
    exps = [jnp.exp(v - vals[0]) for v in vals]
    denom = exps[0] + exps[1] + exps[2] + exps[3]
    idx_ref[...] = jnp.concatenate(idxs, axis=0)
    gate_ref[...] = jnp.concatenate(exps, axis=0) / denom


def _router(x, router_w, router_b):
    t, d = x.shape
    n_e = router_w.shape[1]
    tb = _pick(t, (768, 512, 256, 128))
    return pl.pallas_call(
        _router_body,
        grid=(t // tb,),
        in_specs=[pl.BlockSpec((tb, d), lambda i: (i, 0)),
                  pl.BlockSpec((n_e, d), lambda i: (0, 0)),
                  pl.BlockSpec((n_e, 1), lambda i: (0, 0))],
        out_specs=[pl.BlockSpec((TOP_K, tb), lambda i: (0, i)),
                   pl.BlockSpec((TOP_K, tb), lambda i: (0, i))],
        out_shape=[jax.ShapeDtypeStruct((TOP_K, t), jnp.int32),
                   jax.ShapeDtypeStruct((TOP_K, t), F32)],
        compiler_params=_params("parallel"),
        name="moe_router",
    )(x, router_w.T, router_b.reshape(n_e, 1))


def _row_copy(src_hbm, row, dst, dst_row, sem):
    return pltpu.make_async_copy(src_hbm.at[pl.ds(row, 1), :], dst.at[pl.ds(dst_row, 1), :], sem)


def _gather_body(nused_ref, tok_cur, tok_nxt, x_hbm, o_ref, buf, sem, *, tg):
    i = pl.program_id(0)
    n_used = nused_ref[0]
    slot = i % 2

    def issue(tok_ref, s):
        def body(r, c):
            _row_copy(x_hbm, tok_ref[r], buf.at[s], r, sem.at[s]).start()
            return c
        lax.fori_loop(0, tg, body, 0)

    @pl.when(jnp.logical_and(i == 0, n_used > 0))
    def _():
        issue(tok_cur, 0)

    @pl.when(i + 1 < n_used)
    def _():
        issue(tok_nxt, 1 - slot)

    @pl.when(i < n_used)
    def _():
        pltpu.make_async_copy(x_hbm.at[pl.ds(0, tg), :], buf.at[slot], sem.at[slot]).wait()
        o_ref[...] = buf[slot].astype(BF16)


def _gather_rows(x, slot_tok, n_used_rows, tg):
    t, d = x.shape
    m_pad = slot_tok.shape[0]
    nblk = m_pad // tg
    n_used = (n_used_rows // tg).astype(jnp.int32).reshape(1)
    last = lambda i, nu: jnp.maximum(jnp.minimum(i, nu[0] - 1), 0)
    grid_spec = pltpu.PrefetchScalarGridSpec(
        num_scalar_prefetch=1,
        grid=(nblk,),
        in_specs=[pl.BlockSpec((tg,), lambda i, nu: (last(i, nu),), memory_space=pltpu.SMEM),
                  pl.BlockSpec((tg,), lambda i, nu: (last(i + 1, nu),), memory_space=pltpu.SMEM),
                  pl.BlockSpec(memory_space=pl.ANY)],
        out_specs=pl.BlockSpec((tg, d), lambda i, nu: (last(i, nu), 0)),
        scratch_shapes=[pltpu.VMEM((2, tg, d), F32), pltpu.SemaphoreType.DMA((2,))],
    )
    return pl.pallas_call(
        functools.partial(_gather_body, tg=tg),
        grid_spec=grid_spec,
        out_shape=jax.ShapeDtypeStruct((m_pad, d), BF16),
        compiler_params=_params("arbitrary"),
        name="moe_gather",
    )(n_used, slot_tok, slot_tok, x)


def _expert_body(meta_ref, x_ref, wg_ref, wl_ref, wd_ref, bg_ref, bl_ref, bd_ref, o_ref):
    i = pl.program_id(0)
    j = pl.program_id(1)

    @pl.when(i < meta_ref[0])
    def _():
        x = x_ref[...]
        glu = jnp.dot(x, wg_ref[0], preferred_element_type=F32) + bg_ref[0]
        lin = jnp.dot(x, wl_ref[0], preferred_element_type=F32) + bl_ref[0]
        glu = jnp.minimum(glu, SWIGLU_LIMIT)
        lin = jnp.clip(lin, -SWIGLU_LIMIT, SWIGLU_LIMIT)
        act = glu * jax.nn.sigmoid(SWIGLU_ALPHA * glu) * (lin + 1.0)
        part = jnp.dot(act.astype(BF16), wd_ref[0], preferred_element_type=F32)

        @pl.when(j == 0)
        def _():
            o_ref[...] = part + bd_ref[0]

        @pl.when(j > 0)
        def _():
            o_ref[...] += part


def _expert_ffn(xs, block_expert, n_used_blocks, w_gu, b_gu, w_d, b_d, tm):
    m_pad, d = xs.shape
    n_e, _, two_f = w_gu.shape
    d_ff = two_f // 2
    tf = _pick(d_ff, (512, 256, 128))
    n_f = d_ff // tf
    nblk = m_pad // tm
    meta = jnp.concatenate([n_used_blocks.reshape(1).astype(jnp.int32), block_expert.astype(jnp.int32)])

    def blk(i, meta):
        return jnp.maximum(jnp.minimum(i, meta[0] - 1), 0)

    def fj(i, j, meta):
        return jnp.where(i < meta[0], j, n_f - 1)

    def expert(i, meta):
        return meta[1 + blk(i, meta)]

    grid_spec = pltpu.PrefetchScalarGridSpec(
        num_scalar_prefetch=1,
        grid=(nblk, n_f),
        in_specs=[
            pl.BlockSpec((tm, d), lambda i, j, m: (blk(i, m), 0)),
            pl.BlockSpec((1, d, tf), lambda i, j, m: (expert(i, m), 0, fj(i, j, m))),
            pl.BlockSpec((1, d, tf), lambda i, j, m: (expert(i, m), 0, n_f + fj(i, j, m))),
            pl.BlockSpec((1, tf, d), lambda i, j, m: (expert(i, m), fj(i, j, m), 0)),
            pl.BlockSpec((1, 1, tf), lambda i, j, m: (expert(i, m), 0, fj(i, j, m))),
            pl.BlockSpec((1, 1, tf), lambda i, j, m: (expert(i, m), 0, n_f + fj(i, j, m))),
            pl.BlockSpec((1, 1, d), lambda i, j, m: (expert(i, m), 0, 0)),
        ],
        out_specs=pl.BlockSpec((tm, d), lambda i, j, m: (blk(i, m), 0)),
    )
    return pl.pallas_call(
        _expert_body,
        grid_spec=grid_spec,
        out_shape=jax.ShapeDtypeStruct((m_pad, d), F32),
        compiler_params=_params("arbitrary", "arbitrary"),
        name="moe_expert_ffn",
    )(meta, xs, w_gu, w_gu, w_d, b_gu.reshape(n_e, 1, two_f), b_gu.reshape(n_e, 1, two_f),
      b_d.reshape(n_e, 1, d))


def _combine_body(pos_cur, pos_nxt, y_hbm, gate_ref, xres_ref, g_ref, b_ref, of_ref, ob_ref, buf, sem,
                  *, tb, alpha):
    i = pl.program_id(0)
    nblk = pl.num_programs(0)
    slot = i % 2

    def issue(pos_ref, s):
        for k in range(TOP_K):
            def body(t, c, k=k):
                _row_copy(y_hbm, pos_ref[t * TOP_K + k], buf.at[s, k], t, sem.at[s]).start()
                return c
            lax.fori_loop(0, tb, body, 0)

    @pl.when(i == 0)
    def _():
        issue(pos_cur, 0)

    @pl.when(i + 1 < nblk)
    def _():
        issue(pos_nxt, 1 - slot)

    for k in range(TOP_K):
        pltpu.make_async_copy(y_hbm.at[pl.ds(0, tb), :], buf.at[slot, k], sem.at[slot]).wait()
    gate = gate_ref[...]
    f = gate[:, 0:1] * buf[slot, 0]
    for k in range(1, TOP_K):
        f = f + gate[:, k:k + 1] * buf[slot, k]
    y = _layer_norm_rows(alpha * xres_ref[...] + f, g_ref[...], b_ref[...])
    of_ref[...] = y
    ob_ref[...] = y.astype(BF16)


def _combine_ln(y_sorted, pos, gates, xres, g, b, alpha):
    t, d = xres.shape
    tb = _pick(t, (256, 128, 64, 32, 16, 8))
    nblk = t // tb
    row = lambda i: (i, 0)
    fixed = lambda i: (0, 0)
    pos_flat = pos.reshape(t * TOP_K)
    return pl.pallas_call(
        functools.partial(_combine_body, tb=tb, alpha=alpha),
        grid=(nblk,),
        in_specs=[pl.BlockSpec((tb * TOP_K,), lambda i: (i,), memory_space=pltpu.SMEM),
                  pl.BlockSpec((tb * TOP_K,), lambda i: (jnp.minimum(i + 1, nblk - 1),), memory_space=pltpu.SMEM),
                  pl.BlockSpec(memory_space=pl.ANY),
                  pl.BlockSpec((tb, TOP_K), row),
                  pl.BlockSpec((tb, d), row),
                  pl.BlockSpec((1, d), fixed),
                  pl.BlockSpec((1, d), fixed)],
        out_specs=[pl.BlockSpec((tb, d), row), pl.BlockSpec((tb, d), row)],
        out_shape=[jax.ShapeDtypeStruct((t, d), F32), jax.ShapeDtypeStruct((t, d), BF16)],
        scratch_shapes=[pltpu.VMEM((2, TOP_K, tb, d), F32), pltpu.SemaphoreType.DMA((2,))],
        compiler_params=_params("arbitrary"),
        name="moe_combine_ln",
    )(pos_flat, pos_flat, y_sorted, gates, xres, g.reshape(1, d), b.reshape(1, d))


def _moe_ln(x, x_unused_bf16, router_w, router_b, w_gu, b_gu, w_d, b_d, g, b, alpha):
    del x_unused_bf16
    t, d = x.shape
    n_e = router_w.shape[1]
    m = t * TOP_K
    tm = _pick(m, (512, 256, 128, 64, 32, 16))
    tg = _pick(tm, (256, 128, 64, 32, 16))
    idx_kt, gate_kt = _router(x, router_w, router_b)
    flat_e = idx_kt.T.reshape(m)
    order = jnp.argsort(flat_e, stable=True).astype(jnp.int32)
    e_sorted = flat_e[order]
    counts = jnp.sum(flat_e[:, None] == jnp.arange(n_e, dtype=jnp.int32)[None, :], axis=0, dtype=jnp.int32)
    padded = (counts + tm - 1) // tm * tm
    pad_end = jnp.cumsum(padded)
    pad_start = pad_end - padded
    start = jnp.cumsum(counts) - counts
    dest_sorted = pad_start[e_sorted] + jnp.arange(m, dtype=jnp.int32) - start[e_sorted]
    n_blocks = m // tm + n_e
    m_pad = n_blocks * tm
    slot_tok = jnp.zeros((m_pad,), jnp.int32).at[dest_sorted].set(order // TOP_K)
    pos = jnp.zeros((m,), jnp.int32).at[order].set(dest_sorted)
    block_expert = jnp.minimum(
        jnp.searchsorted(pad_end, jnp.arange(n_blocks, dtype=jnp.int32) * tm, side='right'), n_e - 1)
    n_used_rows = pad_end[-1]
    xs = _gather_rows(x, slot_tok, n_used_rows, tg)
    ys = _expert_ffn(xs, block_expert, n_used_rows // tm, w_gu, b_gu, w_d, b_d, tm)
    return _combine_ln(ys, pos.reshape(t, TOP_K), gate_kt.T, x, g, b, alpha)


def _split_cols(t, sizes):
    idx = [int(i) for i in np.cumsum(sizes)[:-1]]
    return jnp.split(t, idx, axis=-1)


def _l2norm(t):
    return t * lax.rsqrt(jnp.sum(t * t, axis=-1, keepdims=True) + 1e-6)


def _gated_rms_norm(o, z, w):
    o = o * lax.rsqrt(jnp.mean(o * o, axis=-1, keepdims=True) + RMS_EPS)
    return o * w.astype(F32) * jax.nn.silu(z)


def _causal_conv(u, prev, w):
    seq = u.shape[1]
    ucat = jnp.concatenate([prev.astype(u.dtype), u], axis=1)
    out = sum(ucat[:, j:j + seq] * w[j].astype(u.dtype) for j in range(CONV_W))
    return out, ucat[:, seq:]


def _to_chunks(t, blk):
    b, l = t.shape[:2]
    t = t.reshape((b, l // blk, blk) + t.shape[2:])
    return jnp.moveaxis(jnp.moveaxis(t, 1, 0), 2, 3)


def _from_chunks(o):
    o = jnp.moveaxis(jnp.moveaxis(o, 3, 2), 0, 1)
    return o.reshape((o.shape[0], o.shape[1] * o.shape[2]) + o.shape[3:])


def _gdn_scan(q, k, v, log_a, beta, s0):
    blk = min(CHUNK, q.shape[1])
    dv = v.shape[-1]
    incl = jnp.tril(jnp.ones((blk, blk), bool))
    strict = jnp.tril(jnp.ones((blk, blk), bool), -1)

    def step(s, inp):
        qc, kc, vc, ac, bc = inp
        g = jnp.cumsum(ac, axis=-1)
        gamma = jnp.exp(jnp.where(incl, g[..., :, None] - g[..., None, :], -jnp.inf))
        a_mat = jnp.where(strict, bc[..., :, None] * jnp.einsum('bhtd,bhsd->bhts', kc, kc) * gamma, 0.0)
        rhs = jnp.concatenate([vc * bc[..., None], kc * (bc * jnp.exp(g))[..., None]], axis=-1)
        sol = lax.linalg.triangular_solve(a_mat, rhs, left_side=True, lower=True, unit_diagonal=True)
        u, w = sol[..., :dv], sol[..., dv:]
        delta = u - jnp.einsum('bhtd,bhde->bhte', w, s)
        o = (jnp.einsum('bhtd,bhde->bhte', qc * jnp.exp(g)[..., None], s)
             + jnp.einsum('bhts,bhse->bhte', jnp.einsum('bhtd,bhsd->bhts', qc, kc) * gamma, delta))
        g_end = g[..., -1:]
        s = (jnp.exp(g_end)[..., None] * s
             + jnp.einsum('bhsd,bhse->bhde', kc * jnp.exp(g_end - g)[..., None], delta))
        return s, o

    s, o = lax.scan(step, s0, tuple(_to_chunks(t, blk) for t in (q, k, v, log_a, beta)))
    return _from_chunks(o), s


def _ssd_scan(xdt, a, bm, cm, s0):
    blk = min(CHUNK, xdt.shape[1])
    incl = jnp.tril(jnp.ones((blk, blk), bool))

    def step(s, inp):
        xc, ac, bc, cc = inp
        cum = jnp.cumsum(ac, axis=2)
        seg = jnp.exp(jnp.where(incl[:, :, None], cum[:, :, :, None, :] - cum[:, :, None, :, :], -jnp.inf))
        cb = jnp.einsum('bgtn,bgsn->bgts', cc, bc)
        y = (jnp.einsum('bgts,bgtsr,bgsrp->bgtrp', cb, seg, xc)
             + jnp.einsum('bgtn,bgrnp->bgtrp', cc, s) * jnp.exp(cum)[..., None])
        c_end = cum[:, :, -1:]
        s = (jnp.exp(c_end[:, :, 0])[..., None, None] * s
             + jnp.einsum('bgsn,bgsrp->bgrnp', bc, xc * jnp.exp(c_end - cum)[..., None]))
        return s, y

    s, y = lax.scan(step, s0, tuple(_to_chunks(t, blk) for t in (xdt, a, bm, cm)))
    return _from_chunks(y), s


def _gla_scan(q, k, v, log_f, s0):
    blk = min(CHUNK, q.shape[1])
    incl = jnp.tril(jnp.ones((blk, blk), bool))

    def step(s, inp):
        qc, kc, vc, fc = inp
        b = jnp.cumsum(fc, axis=2)
        dec = jnp.exp(jnp.where(incl[:, :, None], b[:, :, :, None, :] - b[:, :, None, :, :], -jnp.inf))
        attn = jnp.einsum('bhtd,bhsd,bhtsd->bhts', qc, kc, dec)
        o = jnp.einsum('bhtd,bhde->bhte', qc * jnp.exp(b), s) + jnp.einsum('bhts,bhse->bhte', attn, vc)
        b_end = b[:, :, -1:]
        s = (jnp.exp(b_end[:, :, 0])[..., None] * s
             + jnp.einsum('bhsd,bhse->bhde', kc * jnp.exp(b_end - b), vc))
        return s, o

    s, o = lax.scan(step, s0, tuple(_to_chunks(t, blk) for t in (q, k, v, log_f)))
    return _from_chunks(o), s


def _mixer_ab_core(proj, dims, conv_g, s_g, conv_s, s_s, conv_gdn_w, gdn_a_log, gdn_dt_bias, gdn_norm_w,
                   conv_ssd_w, conv_ssd_b, ssd_a_log, ssd_dt_bias, ssd_d, ssd_norm_w):
    h_a, dk_a, dv_a, h_b, p_b, n_b, d_inner = dims
    r_b = h_b // G_B
    qkv_a = h_a * (2 * dk_a + dv_a)
    xbc_b = d_inner + 2 * G_B * n_b
    bsz, seq, _ = proj.shape
    qkv, z_a, z_b, xbc, b_raw, a_raw, dt_raw, _ = _split_cols(
        proj, [qkv_a, h_a * dv_a, d_inner, xbc_b, h_a, h_a, h_b, proj.shape[-1]])
    qkv, conv_g_new = _causal_conv(qkv, conv_g, conv_gdn_w)
    q, k, v = _split_cols(jax.nn.silu(qkv), [h_a * dk_a, h_a * dk_a, h_a * dv_a])
    q = _l2norm(q.reshape(bsz, seq, h_a, dk_a)) * dk_a ** -0.5
    k = _l2norm(k.reshape(bsz, seq, h_a, dk_a))
    v = v.reshape(bsz, seq, h_a, dv_a)
    beta = jax.nn.sigmoid(b_raw)
    log_a = -jnp.exp(gdn_a_log.astype(F32)) * jax.nn.softplus(a_raw + gdn_dt_bias.astype(F32))
    o_a, s_g_new = _gdn_scan(q, k, v, log_a, beta, s_g.astype(F32))
    o_a = _gated_rms_norm(o_a, z_a.reshape(bsz, seq, h_a, dv_a), gdn_norm_w).reshape(bsz, seq, h_a * dv_a)
    xbc, conv_s_new = _causal_conv(xbc, conv_s, conv_ssd_w)
    xs, bm, cm = _split_cols(jax.nn.silu(xbc + conv_ssd_b.astype(F32)), [d_inner, G_B * n_b, G_B * n_b])
    dt = jax.nn.softplus(dt_raw + ssd_dt_bias.astype(F32)).reshape(bsz, seq, G_B, r_b)
    xh = xs.reshape(bsz, seq, G_B, r_b, p_b)
    a = -jnp.exp(ssd_a_log.astype(F32)).reshape(G_B, r_b) * dt
    y, s_s_new = _ssd_scan(xh * dt[..., None], a, bm.reshape(bsz, seq, G_B, n_b), cm.reshape(bsz, seq, G_B, n_b),
                           s_s.astype(F32).reshape(bsz, G_B, r_b, n_b, p_b))
    y = y + ssd_d.astype(F32).reshape(G_B, r_b, 1) * xh
    y = (y.reshape(bsz, seq, d_inner) * jax.nn.silu(z_b)).reshape(bsz, seq, G_B, d_inner // G_B)
    y = (y * lax.rsqrt(jnp.mean(y * y, axis=-1, keepdims=True) + RMS_EPS)).reshape(bsz, seq, d_inner)
    y = y * ssd_norm_w.astype(F32)
    mix = jnp.concatenate([o_a, y], axis=-1)
    return mix, conv_g_new, s_g_new, conv_s_new, s_s_new.reshape(bsz, h_b, n_b, p_b)


def _mixer_c_core(proj, dims, lb, s_prev, norm_w):
    h_c, exp_c, hd_c = dims
    bsz, seq, _ = proj.shape
    q, f, i, g = _split_cols(proj, [h_c * exp_c, h_c * exp_c, h_c * hd_c, h_c * hd_c])
    sig_f = jax.nn.sigmoid(f)
    log_f = jnp.log(lb + (1.0 - lb) * sig_f)
    k = (1.0 - lb) * (1.0 - sig_f)
    o, s_new = _gla_scan(jax.nn.silu(q).reshape(bsz, seq, h_c, exp_c) * exp_c ** -0.5,
                         k.reshape(bsz, seq, h_c, exp_c), i.reshape(bsz, seq, h_c, hd_c),
                         log_f.reshape(bsz, seq, h_c, exp_c), s_prev.astype(F32))
    o = _gated_rms_norm(o, g.reshape(bsz, seq, h_c, hd_c), norm_w).reshape(bsz, seq, h_c * hd_c)
    return o, s_new


def kernel(x_prompt, x_sample, cache_conv_gdn, state_gdn, cache_conv_ssd, state_ssd, state_hgrn, w_in_ab, conv_gdn_w, gdn_a_log, gdn_dt_bias, gdn_norm_w, conv_ssd_w, conv_ssd_b, ssd_a_log, ssd_dt_bias, ssd_d, ssd_norm_w, w_out_ab, w_in_c, hgrn_lower_bounds, hgrn_norm_w, w_out_c, router_w, router_b, expert_w_gate_up, expert_b_gate_up, expert_w_down, expert_b_down, ln_mix_g, ln_mix_b, ln_ffn_g, ln_ffn_b):
    bp, lp, d = x_prompt.shape
    bs, ls, _ = x_sample.shape
    depth = ln_mix_g.shape[0]
    n_pairs = depth // 2
    alpha = float((2 * depth) ** 0.25)
    tp, ts = bp * lp, bs * ls
    h_a, dv_a = gdn_a_log.shape[1], gdn_norm_w.shape[1]
    qkv_a = conv_gdn_w.shape[2]
    dk_a = (qkv_a // h_a - dv_a) // 2
    h_b, d_inner = ssd_a_log.shape[1], ssd_norm_w.shape[1]
    p_b = d_inner // h_b
    xbc_b = conv_ssd_w.shape[2]
    n_b = (xbc_b - d_inner) // (2 * G_B)
    hd_c = hgrn_norm_w.shape[1]
    h_c = d // hd_c
    exp_c = hgrn_lower_bounds.shape[1] // h_c
    dims_ab = (h_a, dk_a, dv_a, h_b, p_b, n_b, d_inner)
    dims_c = (h_c, exp_c, hd_c)

    big = qkv_a + h_a * dv_a + d_inner + xbc_b
    small = 2 * h_a + h_b
    small_pad = -(-small // (2 * V7X_MXU_DIM)) * (2 * V7X_MXU_DIM)
    c0 = qkv_a
    w_qkv, w_b, w_a, w_za, w_zb, w_xbc, w_dt = _split_cols(
        w_in_ab, [qkv_a, h_a, h_a, h_a * dv_a, d_inner, xbc_b, h_b])
    del c0, big
    w_in_ab_r = jnp.concatenate(
        [w_qkv, w_za, w_zb, w_xbc, w_b, w_a, w_dt,
         jnp.zeros(w_in_ab.shape[:2] + (small_pad - small,), w_in_ab.dtype)], axis=-1).astype(BF16)
    w_in_c_b = w_in_c.astype(BF16)
    w_out_ab_b = w_out_ab.astype(BF16)
    w_out_c_b = w_out_c.astype(BF16)
    w_gu_b = expert_w_gate_up.astype(BF16)
    w_d_b = expert_w_down.astype(BF16)

    lb_w = jax.nn.softmax(hgrn_lower_bounds.astype(F32), axis=0)
    lower_bounds = jnp.cumsum(lb_w, axis=0) - lb_w[0]

    x = jnp.concatenate([x_prompt.reshape(tp, d), x_sample.reshape(ts, d)], axis=0)
    xb = x.astype(BF16)
    zeros = lambda *shape: jnp.zeros(shape, F32)
    new = {key: [] for key in ('cg_p', 'sg_p', 'cs_p', 'ss_p', 'sh_p', 'cg_s', 'sg_s', 'cs_s', 'ss_s', 'sh_s')}
    for layer in range(depth):
        j = layer // 2
        if layer % 2 == 0:
            proj = _matmul(xb, w_in_ab_r[j])
            wts = (conv_gdn_w[j], gdn_a_log[j], gdn_dt_bias[j], gdn_norm_w[j], conv_ssd_w[j], conv_ssd_b[j],
                   ssd_a_log[j], ssd_dt_bias[j], ssd_d[j], ssd_norm_w[j])
            mix_p, cg, sg, cs, ss = _mixer_ab_core(
                proj[:tp].reshape(bp, lp, -1), dims_ab,
                zeros(bp, CONV_W - 1, qkv_a), zeros(bp, h_a, dk_a, dv_a),
                zeros(bp, CONV_W - 1, xbc_b), zeros(bp, h_b, n_b, p_b), *wts)
            for key, val in zip(('cg_p', 'sg_p', 'cs_p', 'ss_p'), (cg, sg, cs, ss)):
                new[key].append(val)
            mix_s, cg, sg, cs, ss = _mixer_ab_core(
                proj[tp:].reshape(bs, ls, -1), dims_ab,
                cache_conv_gdn[j], state_gdn[j], cache_conv_ssd[j], state_ssd[j], *wts)
            for key, val in zip(('cg_s', 'sg_s', 'cs_s', 'ss_s'), (cg, sg, cs, ss)):
                new[key].append(val)
            w_out = w_out_ab_b[j]
        else:
            proj = _matmul(xb, w_in_c_b[j])
            mix_p, sh = _mixer_c_core(proj[:tp].reshape(bp, lp, -1), dims_c, lower_bounds[j],
                                      zeros(bp, h_c, exp_c, hd_c), hgrn_norm_w[j])
            new['sh_p'].append(sh)
            mix_s, sh = _mixer_c_core(proj[tp:].reshape(bs, ls, -1), dims_c, lower_bounds[j],
                                      state_hgrn[j], hgrn_norm_w[j])
            new['sh_s'].append(sh)
            w_out = w_out_c_b[j]
        mix = jnp.concatenate([mix_p.reshape(tp, -1), mix_s.reshape(ts, -1)], axis=0).astype(BF16)
        x, xb = _outproj_ln(mix, w_out, x, ln_mix_g[layer], ln_mix_b[layer], alpha)
        x, xb = _moe_ln(x, xb, router_w[layer], router_b[layer], w_gu_b[layer], expert_b_gate_up[layer],
                        w_d_b[layer], expert_b_down[layer], ln_ffn_g[layer], ln_ffn_b[layer], alpha)
    st = {key: jnp.stack(val) for key, val in new.items()}
    return (x[:tp].reshape(bp, lp, d), x[tp:].reshape(bs, ls, d),
            st['cg_p'], st['sg_p'], st['cs_p'], st['ss_p'], st['sh_p'],
            st['cg_s'], st['sg_s'], st['cs_s'], st['ss_s'], st['sh_s'])
```

```python
import functools

import jax
import jax.numpy as jnp
import numpy as np
from jax import lax
from jax.experimental import pallas as pl
from jax.experimental.pallas import tpu as pltpu

F32 = jnp.float32
BF16 = jnp.bfloat16

CHUNK = 64
CONV_W = 4
G_B = 2
TOP_K = 4
SWIGLU_LIMIT = 7.0
SWIGLU_ALPHA = 1.702
LN_EPS = 1e-5
RMS_EPS = 1e-6

V7X_LANES = 128
V7X_MXU_DIM = 256
V7X_VMEM_LIMIT_BYTES = 56 * 1024 * 1024


def _pick(n, candidates):
    for c in candidates:
        if n % c == 0:
            return c
    return n


def _params(*sem):
    return pltpu.CompilerParams(dimension_semantics=sem, vmem_limit_bytes=V7X_VMEM_LIMIT_BYTES)


def _matmul_body(x_ref, w_ref, o_ref):
    o_ref[...] = jnp.dot(x_ref[...], w_ref[...], preferred_element_type=F32)


def _matmul(x, w):
    m, k = x.shape
    n = w.shape[1]
    tm = _pick(m, (768, 512, 256, 128, 64, 32, 16, 8))
    tn = _pick(n, (512, 256, 128))
    return pl.pallas_call(
        _matmul_body,
        grid=(m // tm, n // tn),
        in_specs=[pl.BlockSpec((tm, k), lambda i, j: (i, 0)),
                  pl.BlockSpec((k, tn), lambda i, j: (0, j))],
        out_specs=pl.BlockSpec((tm, tn), lambda i, j: (i, j)),
        out_shape=jax.ShapeDtypeStruct((m, n), F32),
        compiler_params=_params("parallel", "arbitrary"),
        name="dense_proj",
    )(x, w)


def _layer_norm_rows(y, g, b):
    mu = jnp.mean(y, axis=-1, keepdims=True)
    yc = y - mu
    var = jnp.mean(yc * yc, axis=-1, keepdims=True)
    return yc * lax.rsqrt(var + LN_EPS) * g + b


def _outproj_ln_body(*refs, n_in, alpha):
    mix_refs, w_refs = refs[:n_in], refs[n_in:2 * n_in]
    xres_ref, g_ref, b_ref, of_ref, ob_ref = refs[2 * n_in:]
    h = jnp.dot(mix_refs[0][...], w_refs[0][...], preferred_element_type=F32)
    for mix_ref, w_ref in zip(mix_refs[1:], w_refs[1:]):
        h = h + jnp.dot(mix_ref[...], w_ref[...], preferred_element_type=F32)
    y = _layer_norm_rows(alpha * xres_ref[...] + h, g_ref[...], b_ref[...])
    of_ref[...] = y
    ob_ref[...] = y.astype(BF16)


def _outproj_ln(mixes, ws, xres, g, b, alpha):
    m, d = xres.shape
    tm = _pick(m, (384, 256, 128, 64, 32, 16, 8))
    row = lambda i: (i, 0)
    fixed = lambda i: (0, 0)
    return pl.pallas_call(
        functools.partial(_outproj_ln_body, n_in=len(mixes), alpha=alpha),
        grid=(m // tm,),
        in_specs=([pl.BlockSpec((tm, mix.shape[1]), row) for mix in mixes]
                  + [pl.BlockSpec(w.shape, fixed) for w in ws]
                  + [pl.BlockSpec((tm, d), row), pl.BlockSpec((1, d), fixed), pl.BlockSpec((1, d), fixed)]),
        out_specs=[pl.BlockSpec((tm, d), row), pl.BlockSpec((tm, d), row)],
        out_shape=[jax.ShapeDtypeStruct((m, d), F32), jax.ShapeDtypeStruct((m, d), BF16)],
        compiler_params=_params("parallel"),
        name="outproj_ln",
    )(*mixes, *ws, xres, g.reshape(1, d), b.reshape(1, d))


def _router_body(x_ref, wt_ref, b_ref, idx_ref, gate_ref):
    logits = lax.dot_general(wt_ref[...], x_ref[...], (((1,), (1,)), ((), ())),
                             precision=lax.Precision.HIGHEST, preferred_element_type=F32)
    logits = logits + b_ref[...]
    n_e = logits.shape[0]
    iota = lax.broadcasted_iota(jnp.int32, logits.shape, 0)
    vals, idxs = [], []
    for _ in range(TOP_K):
        m = jnp.max(logits, axis=0, keepdims=True)
        sel = jnp.min(jnp.where(logits == m, iota, n_e), axis=0, keepdims=True)
        vals.append(m)
        idxs.append(sel)
        logits = jnp.where(iota == sel, -jnp.inf, logits)
    exps = [jnp.exp(v - vals[0]) for v in vals]
    denom = exps[0] + exps[1] + exps[2] + exps[3]
    idx_ref[...] = jnp.concatenate(idxs, axis=0)
    gate_ref[...] = jnp.concatenate(exps, axis=0) / denom


def _router(x, router_w, router_b):
    t, d = x.shape
    n_e = router_w.shape[1]
    tb = _pick(t, (768, 512, 256, 128))
    return pl.pallas_call(
        _router_body,
        grid=(t // tb,),
        in_specs=[pl.BlockSpec((tb, d), lambda i: (i, 0)),
                  pl.BlockSpec((n_e, d), lambda i: (0, 0)),
                  pl.BlockSpec((n_e, 1), lambda i: (0, 0))],
        out_specs=[pl.BlockSpec((TOP_K, tb), lambda i: (0, i)),
                   pl.BlockSpec((TOP_K, tb), lambda i: (0, i))],
        out_shape=[jax.ShapeDtypeStruct((TOP_K, t), jnp.int32),
                   jax.ShapeDtypeStruct((TOP_K, t), F32)],
        compiler_params=_params("parallel"),
        name="moe_router",
    )(x, router_w.T, router_b.reshape(n_e, 1))


def _row_copy(src_hbm, row, dst, dst_row, sem):
    return pltpu.make_async_copy(src_hbm.at[pl.ds(row, 1), :], dst.at[pl.ds(dst_row, 1), :], sem)


def _gather_body(nused_ref, tok_cur, tok_nxt, x_hbm, o_ref, buf, sem, *, tg):
    i = pl.program_id(0)
    n_used = nused_ref[0]
    slot = i % 2

    def issue(tok_ref, s):
        def body(r, c):
            _row_copy(x_hbm, tok_ref[r], buf.at[s], r, sem.at[s]).start()
            return c
        lax.fori_loop(0, tg, body, 0)

    @pl.when(jnp.logical_and(i == 0, n_used > 0))
    def _():
        issue(tok_cur, 0)

    @pl.when(i + 1 < n_used)
    def _():
        issue(tok_nxt, 1 - slot)

    @pl.when(i < n_used)
    def _():
        pltpu.make_async_copy(x_hbm.at[pl.ds(0, tg), :], buf.at[slot], sem.at[slot]).wait()
        o_ref[...] = buf[slot].astype(BF16)

    @pl.when(i >= n_used)
    def _():
        o_ref[...] = jnp.zeros_like(o_ref)


def _gather_rows(x, slot_tok, n_used_rows, tg):
    t, d = x.shape
    m_pad = slot_tok.shape[0]
    nblk = m_pad // tg
    n_used = (n_used_rows // tg).astype(jnp.int32).reshape(1)
    last = lambda i, nu: jnp.maximum(jnp.minimum(i, nu[0] - 1), 0)
    grid_spec = pltpu.PrefetchScalarGridSpec(
        num_scalar_prefetch=1,
        grid=(nblk,),
        in_specs=[pl.BlockSpec((tg,), lambda i, nu: (last(i, nu),), memory_space=pltpu.SMEM),
                  pl.BlockSpec((tg,), lambda i, nu: (last(i + 1, nu),), memory_space=pltpu.SMEM),
                  pl.BlockSpec(memory_space=pl.ANY)],
        out_specs=pl.BlockSpec((tg, d), lambda i, nu: (i, 0)),
        scratch_shapes=[pltpu.VMEM((2, tg, d), F32), pltpu.SemaphoreType.DMA((2,))],
    )
    return pl.pallas_call(
        functools.partial(_gather_body, tg=tg),
        grid_spec=grid_spec,
        out_shape=jax.ShapeDtypeStruct((m_pad, d), BF16),
        compiler_params=_params("arbitrary"),
        name="moe_gather",
    )(n_used, slot_tok, slot_tok, x)


def _expert_body(meta_ref, x_ref, wg_ref, wl_ref, wd_ref, bg_ref, bl_ref, bd_ref, o_ref):
    i = pl.program_id(0)
    j = pl.program_id(1)

    @pl.when(i < meta_ref[0])
    def _():
        x = x_ref[...]
        glu = jnp.dot(x, wg_ref[0], preferred_element_type=F32) + bg_ref[0]
        lin = jnp.dot(x, wl_ref[0], preferred_element_type=F32) + bl_ref[0]
        glu = jnp.minimum(glu, SWIGLU_LIMIT)
        lin = jnp.clip(lin, -SWIGLU_LIMIT, SWIGLU_LIMIT)
        act = glu * jax.nn.sigmoid(SWIGLU_ALPHA * glu) * (lin + 1.0)
        part = jnp.dot(act.astype(BF16), wd_ref[0], preferred_element_type=F32)

        @pl.when(j == 0)
        def _():
            o_ref[...] = part + bd_ref[0]

        @pl.when(j > 0)
        def _():
            o_ref[...] += part

    @pl.when(jnp.logical_and(i >= meta_ref[0], j == 0))
    def _():
        o_ref[...] = jnp.zeros_like(o_ref)


def _expert_ffn(xs, block_expert, n_used_blocks, w_gu, b_gu, w_d, b_d, tm):
    m_pad, d = xs.shape
    n_e, _, two_f = w_gu.shape
    d_ff = two_f // 2
    tf = _pick(d_ff, (512, 256, 128))
    n_f = d_ff // tf
    nblk = m_pad // tm
    meta = jnp.concatenate([n_used_blocks.reshape(1).astype(jnp.int32), block_expert.astype(jnp.int32)])

    def blk(i, meta):
        return jnp.maximum(jnp.minimum(i, meta[0] - 1), 0)

    def fj(i, j, meta):
        return jnp.where(i < meta[0], j, n_f - 1)

    def expert(i, meta):
        return meta[1 + blk(i, meta)]

    grid_spec = pltpu.PrefetchScalarGridSpec(
        num_scalar_prefetch=1,
        grid=(nblk, n_f),
        in_specs=[
            pl.BlockSpec((tm, d), lambda i, j, m: (blk(i, m), 0)),
            pl.BlockSpec((1, d, tf), lambda i, j, m: (expert(i, m), 0, fj(i, j, m))),
            pl.BlockSpec((1, d, tf), lambda i, j, m: (expert(i, m), 0, n_f + fj(i, j, m))),
            pl.BlockSpec((1, tf, d), lambda i, j, m: (expert(i, m), fj(i, j, m), 0)),
            pl.BlockSpec((1, 1, tf), lambda i, j, m: (expert(i, m), 0, fj(i, j, m))),
            pl.BlockSpec((1, 1, tf), lambda i, j, m: (expert(i, m), 0, n_f + fj(i, j, m))),
            pl.BlockSpec((1, 1, d), lambda i, j, m: (expert(i, m), 0, 0)),
        ],
        out_specs=pl.BlockSpec((tm, d), lambda i, j, m: (i, 0)),
    )
    return pl.pallas_call(
        _expert_body,
        grid_spec=grid_spec,
        out_shape=jax.ShapeDtypeStruct((m_pad, d), F32),
        compiler_params=_params("arbitrary", "arbitrary"),
        name="moe_expert_ffn",
    )(meta, xs, w_gu, w_gu, w_d, b_gu.reshape(n_e, 1, two_f), b_gu.reshape(n_e, 1, two_f),
      b_d.reshape(n_e, 1, d))


def _combine_body(pos_cur, pos_nxt, y_hbm, gate_ref, xres_ref, g_ref, b_ref, of_ref, ob_ref, buf, sem,
                  *, tb, alpha):
    i = pl.program_id(0)
    nblk = pl.num_programs(0)
    slot = i % 2

    def issue(pos_ref, s):
        for k in range(TOP_K):
            def body(t, c, k=k):
                _row_copy(y_hbm, pos_ref[t * TOP_K + k], buf.at[s, k], t, sem.at[s]).start()
                return c
            lax.fori_loop(0, tb, body, 0)

    @pl.when(i == 0)
    def _():
        issue(pos_cur, 0)

    @pl.when(i + 1 < nblk)
    def _():
        issue(pos_nxt, 1 - slot)

    for k in range(TOP_K):
        pltpu.make_async_copy(y_hbm.at[pl.ds(0, tb), :], buf.at[slot, k], sem.at[slot]).wait()
    gate = gate_ref[...]
    f = gate[:, 0:1] * buf[slot, 0]
    for k in range(1, TOP_K):
        f = f + gate[:, k:k + 1] * buf[slot, k]
    y = _layer_norm_rows(alpha * xres_ref[...] + f, g_ref[...], b_ref[...])
    of_ref[...] = y
    ob_ref[...] = y.astype(BF16)


def _combine_ln(y_sorted, pos, gates, xres, g, b, alpha):
    t, d = xres.shape
    tb = _pick(t, (256, 128, 64, 32, 16, 8))
    nblk = t // tb
    row = lambda i: (i, 0)
    fixed = lambda i: (0, 0)
    pos_flat = pos.reshape(t * TOP_K)
    return pl.pallas_call(
        functools.partial(_combine_body, tb=tb, alpha=alpha),
        grid=(nblk,),
        in_specs=[pl.BlockSpec((tb * TOP_K,), lambda i: (i,), memory_space=pltpu.SMEM),
                  pl.BlockSpec((tb * TOP_K,), lambda i: (jnp.minimum(i + 1, nblk - 1),), memory_space=pltpu.SMEM),
                  pl.BlockSpec(memory_space=pl.ANY),
                  pl.BlockSpec((tb, TOP_K), row),
                  pl.BlockSpec((tb, d), row),
                  pl.BlockSpec((1, d), fixed),
                  pl.BlockSpec((1, d), fixed)],
        out_specs=[pl.BlockSpec((tb, d), row), pl.BlockSpec((tb, d), row)],
        out_shape=[jax.ShapeDtypeStruct((t, d), F32), jax.ShapeDtypeStruct((t, d), BF16)],
        scratch_shapes=[pltpu.VMEM((2, TOP_K, tb, d), F32), pltpu.SemaphoreType.DMA((2,))],
        compiler_params=_params("arbitrary"),
        name="moe_combine_ln",
    )(pos_flat, pos_flat, y_sorted, gates, xres, g.reshape(1, d), b.reshape(1, d))


def _moe_ln(x, x_unused_bf16, router_w, router_b, w_gu, b_gu, w_d, b_d, g, b, alpha):
    del x_unused_bf16
    t, d = x.shape
    n_e = router_w.shape[1]
    m = t * TOP_K
    tm = _pick(m, (512, 256, 128, 64, 32, 16))
    tg = _pick(tm, (256, 128, 64, 32, 16))
    idx_kt, gate_kt = _router(x, router_w, router_b)
    flat_e = idx_kt.T.reshape(m)
    order = jnp.argsort(flat_e, stable=True).astype(jnp.int32)
    e_sorted = flat_e[order]
    counts = jnp.sum(flat_e[:, None] == jnp.arange(n_e, dtype=jnp.int32)[None, :], axis=0, dtype=jnp.int32)
    padded = (counts + tm - 1) // tm * tm
    pad_end = jnp.cumsum(padded)
    pad_start = pad_end - padded
    start = jnp.cumsum(counts) - counts
    dest_sorted = pad_start[e_sorted] + jnp.arange(m, dtype=jnp.int32) - start[e_sorted]
    n_blocks = m // tm + n_e
    m_pad = n_blocks * tm
    slot_tok = jnp.zeros((m_pad,), jnp.int32).at[dest_sorted].set(order // TOP_K)
    pos = jnp.zeros((m,), jnp.int32).at[order].set(dest_sorted)
    block_expert = jnp.minimum(
        jnp.searchsorted(pad_end, jnp.arange(n_blocks, dtype=jnp.int32) * tm, side='right'), n_e - 1)
    n_used_rows = pad_end[-1]
    xs = _gather_rows(x, slot_tok, n_used_rows, tg)
    ys = _expert_ffn(xs, block_expert, n_used_rows // tm, w_gu, b_gu, w_d, b_d, tm)
    return _combine_ln(ys, pos.reshape(t, TOP_K), gate_kt.T, x, g, b, alpha)


def _halving_levels(c):
    n_lev = int(np.log2(c))
    assert 2 ** n_lev == c
    idx = np.arange(c)
    upper = np.zeros((n_lev, c, c), np.float32)
    lower = np.zeros((n_lev, c, c), np.float32)
    mask = np.zeros((n_lev, c, c), np.float32)
    for lev in range(n_lev):
        h = 2 ** lev
        blk = idx // (2 * h)
        is_up = (idx % (2 * h)) >= h
        mid = blk * 2 * h + h
        r = idx[None, :]
        upper[lev] = (is_up[:, None] & (r >= mid[:, None]) & (r <= idx[:, None]))
        lower[lev] = ((~is_up)[:, None] & (r > idx[:, None]) & (r <= mid[:, None] - 1))
        mask[lev] = (is_up[:, None] & (~is_up)[None, :] & (blk[:, None] == blk[None, :]))
    return upper, lower, mask


def _split3(x):
    hi = x.astype(BF16)
    r1 = x - hi.astype(F32)
    mid = r1.astype(BF16)
    lo = (r1 - mid.astype(F32)).astype(BF16)
    return hi, mid, lo


def _sel_dot(sel, x):
    hi, mid, lo = _split3(x)
    return (jnp.dot(sel, hi, preferred_element_type=F32) + jnp.dot(sel, mid, preferred_element_type=F32)
            + jnp.dot(sel, lo, preferred_element_type=F32))


def _dot_nt(a, b):
    return lax.dot_general(a, b, (((1,), (1,)), ((), ())), preferred_element_type=F32)


def _dot_tn(a, b):
    return lax.dot_general(a, b, (((0,), (0,)), ((), ())), preferred_element_type=F32)


def _gla_body(q_ref, f_ref, v_ref, g_ref, lb_ref, nw_ref, s0_ref, sums_ref, mask_ref, o_ref, s_out_ref, st_scr,
              *, c, hb, dk, scale):
    n_lev = mask_ref.shape[0]
    chunk = pl.program_id(2)

    @pl.when(chunk == 0)
    def _():
        for h in range(hb):
            st_scr[h] = s0_ref[0, h].T

    sums = sums_ref[...]
    for h in range(hb):
        sl = slice(h * dk, (h + 1) * dk)
        lb = lb_ref[:, sl]
        sig = jax.nn.sigmoid(f_ref[:, sl])
        log_f = jnp.log(lb + (1.0 - lb) * sig)
        k = (1.0 - lb) * (1.0 - sig)
        q = jax.nn.silu(q_ref[:, sl]) * scale
        v = v_ref[:, sl]
        vb = v.astype(BF16)
        cs = _sel_dot(sums, log_f)
        b_incl = cs[0:c]
        st = st_scr[h]
        o = _dot_nt((q * jnp.exp(b_incl)).astype(BF16), st.astype(BF16))
        attn = jnp.zeros((c, c), F32)
        for lev in range(n_lev):
            qu = (q * jnp.exp(cs[(2 + lev) * c:(3 + lev) * c])).astype(BF16)
            kl = (k * jnp.exp(cs[(2 + n_lev + lev) * c:(3 + n_lev + lev) * c])).astype(BF16)
            attn = attn + mask_ref[lev] * _dot_nt(qu, kl)
        o = o + jnp.dot(attn.astype(BF16), vb, preferred_element_type=F32)
        o = o + jnp.sum(q * k, axis=-1, keepdims=True) * v
        kd = (k * jnp.exp(cs[c:2 * c])).astype(BF16)
        st_scr[h] = st * jnp.exp(b_incl[c - 1:c, :]) + _dot_tn(vb, kd)
        o = o * lax.rsqrt(jnp.mean(o * o, axis=-1, keepdims=True) + RMS_EPS)
        o_ref[:, sl] = (o * nw_ref[...] * jax.nn.silu(g_ref[:, sl])).astype(BF16)

    @pl.when(chunk == pl.num_programs(2) - 1)
    def _():
        for h in range(hb):
            s_out_ref[0, h] = st_scr[h].T


def _gla_mixer(proj, row0, bsz, seq, lb, s0, norm_w):
    _, n_h, dk, dv = s0.shape
    assert dk == dv == V7X_LANES
    c = min(CHUNK, seq)
    n_chunks = seq // c
    hb = _pick(n_h, (4, 2, 1))
    n_hg = n_h // hb
    upper, lower, mask = _halving_levels(c)
    tri = np.tril(np.ones((c, c), np.float32))
    suffix = np.triu(np.ones((c, c), np.float32), 1)
    sums = jnp.asarray(np.concatenate([tri, suffix] + list(upper) + list(lower), axis=0), BF16)
    w = hb * dk
    blk0 = row0 // c
    sec = lambda s: pl.BlockSpec((c, w), lambda b, g, ch: (blk0 + b * n_chunks + ch, s * n_hg + g))
    fixed2 = lambda b, g, ch: (0, 0)
    state_spec = pl.BlockSpec((1, hb, dk, dv), lambda b, g, ch: (b, g, 0, 0))
    out, s_new = pl.pallas_call(
        functools.partial(_gla_body, c=c, hb=hb, dk=dk, scale=float(dk) ** -0.5),
        grid=(bsz, n_hg, n_chunks),
        in_specs=[sec(0), sec(1), sec(2), sec(3),
                  pl.BlockSpec((1, w), lambda b, g, ch: (0, g)),
                  pl.BlockSpec((1, dv), fixed2),
                  state_spec,
                  pl.BlockSpec(sums.shape, fixed2),
                  pl.BlockSpec(mask.shape, lambda b, g, ch: (0, 0, 0))],
        out_specs=[pl.BlockSpec((c, w), lambda b, g, ch: (b * n_chunks + ch, g)), state_spec],
        out_shape=[jax.ShapeDtypeStruct((bsz * seq, n_h * dv), BF16),
                   jax.ShapeDtypeStruct(s0.shape, F32)],
        scratch_shapes=[pltpu.VMEM((hb, dv, dk), F32)],
        compiler_params=_params("parallel", "parallel", "arbitrary"),
        name="gla_mixer",
    )(proj, proj, proj, proj, lb.reshape(1, n_h * dk), norm_w.reshape(1, dv), s0.astype(F32), sums,
      jnp.asarray(mask))
    return out, s_new


def _dot_f32(a, b):
    return jnp.dot(a, b, precision=lax.Precision.HIGHEST, preferred_element_type=F32)


def _conv_silu(u_ref, prev_ref, w_ref, bias, cbuf, first, c):
    @pl.when(first)
    def _():
        cbuf[8 - (CONV_W - 1):8, :] = prev_ref[0]

    cbuf[8:8 + c, :] = u_ref[...]
    acc = w_ref[CONV_W - 1:CONV_W, :] * cbuf[8:8 + c, :]
    for j in range(CONV_W - 1):
        off = 8 - (CONV_W - 1) + j
        acc = acc + w_ref[j:j + 1, :] * cbuf[off:off + c, :]
    cbuf[0:8, :] = cbuf[c:c + 8, :]
    if bias is not None:
        acc = acc + bias
    return jax.nn.silu(acc)


def _gdn_body(q_ref, k_ref, v_ref, z_ref, sm_ref, pq_ref, pk_ref, pv_ref, wq_ref, wk_ref, wv_ref,
              alog_ref, dtb_ref, nw_ref, s0_ref, tri_ref, su_ref, mask_ref,
              o_ref, s_out_ref, s_scr, cq, ck, cv, *, c, hb, dk, dv):
    n_lev = mask_ref.shape[0]
    chunk = pl.program_id(2)
    first = chunk == 0

    @pl.when(first)
    def _():
        s_scr[...] = s0_ref[0]

    qs = _conv_silu(q_ref, pq_ref, wq_ref, None, cq, first, c)
    ks = _conv_silu(k_ref, pk_ref, wk_ref, None, ck, first, c)
    vs = _conv_silu(v_ref, pv_ref, wv_ref, None, cv, first, c)
    small = sm_ref[...]
    log_a_all = -jnp.exp(alog_ref[0]) * jax.nn.softplus(small + dtb_ref[0])
    tri = tri_ref[...]
    g_all = _sel_dot(tri, log_a_all)
    row = lax.broadcasted_iota(jnp.int32, (c, c), 0)
    col = lax.broadcasted_iota(jnp.int32, (c, c), 1)
    incl = row >= col
    eye = (row == col).astype(F32)
    for h in range(hb):
        sl_k = slice(h * dk, (h + 1) * dk)
        sl_v = slice(h * dv, (h + 1) * dv)
        q = qs[:, sl_k]
        k = ks[:, sl_k]
        v = vs[:, sl_v]
        q = q * lax.rsqrt(jnp.sum(q * q, axis=-1, keepdims=True) + 1e-6) * (float(dk) ** -0.5)
        k = k * lax.rsqrt(jnp.sum(k * k, axis=-1, keepdims=True) + 1e-6)
        beta = jax.nn.sigmoid(small[:, h:h + 1])
        g = g_all[:, hb + h:hb + h + 1]
        dec = _sel_dot(tri, log_a_all[:, hb + h:hb + h + 1] * su_ref[...])
        gamma = jnp.where(incl, jnp.exp(dec), 0.0)
        kb = k.astype(BF16)
        a_mat = beta * _dot_nt(kb, kb) * gamma
        t_inv = eye - mask_ref[0] * a_mat
        for lev in range(1, n_lev):
            t_inv = t_inv - _dot_f32(_dot_f32(t_inv, mask_ref[lev] * a_mat), t_inv)
        eg = jnp.exp(g)
        tb = t_inv.astype(BF16)
        u = jnp.dot(tb, (v * beta).astype(BF16), preferred_element_type=F32)
        w = jnp.dot(tb, (k * (beta * eg)).astype(BF16), preferred_element_type=F32)
        s = s_scr[h]
        sb = s.astype(BF16)
        delta = u - jnp.dot(w.astype(BF16), sb, preferred_element_type=F32)
        db = delta.astype(BF16)
        qb = q.astype(BF16)
        o = (jnp.dot((q * eg).astype(BF16), sb, preferred_element_type=F32)
             + jnp.dot((_dot_nt(qb, kb) * gamma).astype(BF16), db, preferred_element_type=F32))
        g_end = g[c - 1:c, :]
        s_scr[h] = jnp.exp(g_end) * s + _dot_tn((k * jnp.exp(g_end - g)).astype(BF16), db)
        o = o * lax.rsqrt(jnp.mean(o * o, axis=-1, keepdims=True) + RMS_EPS)
        o_ref[:, sl_v] = (o * nw_ref[...] * jax.nn.silu(z_ref[:, sl_v])).astype(BF16)

    @pl.when(chunk == pl.num_programs(2) - 1)
    def _():
        s_out_ref[0] = s_scr[...]


def _ssd_body(x_ref, b_ref, c_ref, z_ref, sm_ref, px_ref, pb_ref, pc_ref, wx_ref, wb_ref, wc_ref,
              bx_ref, bb_ref, bc_ref, alog_ref, dtb_ref, d_ref, nw_ref, exp_ref, s0_ref, tri_ref, su_ref,
              o_ref, s_out_ref, s_scr, cx, cb_buf, cc_buf, *, c, n_r, p):
    chunk = pl.program_id(2)
    first = chunk == 0

    @pl.when(first)
    def _():
        for r in range(n_r):
            s_scr[:, r * p:(r + 1) * p] = s0_ref[0, r]

    xs = _conv_silu(x_ref, px_ref, wx_ref, bx_ref[...], cx, first, c)
    bm = _conv_silu(b_ref, pb_ref, wb_ref, bb_ref[...], cb_buf, first, c)
    cm = _conv_silu(c_ref, pc_ref, wc_ref, bc_ref[...], cc_buf, first, c)
    dt_all = jax.nn.softplus(sm_ref[...] + dtb_ref[0])
    a_all = -jnp.exp(alog_ref[0]) * dt_all
    tri = tri_ref[...]
    cum_all = _sel_dot(tri, a_all)
    expand = exp_ref[...]
    widen = lambda t: sum(jnp.dot(part, expand, preferred_element_type=F32) for part in _split3(t))
    dt_x = widen(dt_all)
    cum_x = widen(cum_all)
    cend_x = cum_x[c - 1:c, :]
    xdt = xs * dt_x
    row = lax.broadcasted_iota(jnp.int32, (c, c), 0)
    col = lax.broadcasted_iota(jnp.int32, (c, c), 1)
    incl = row >= col
    cmb = cm.astype(BF16)
    bmb = bm.astype(BF16)
    cb = _dot_nt(cmb, bmb)
    s = s_scr[...]
    y = jnp.dot(cmb, s.astype(BF16), preferred_element_type=F32) * jnp.exp(cum_x)
    xdtb = xdt.astype(BF16)
    parts = []
    for r in range(n_r):
        dec = _sel_dot(tri, a_all[:, r:r + 1] * su_ref[...])
        seg = jnp.where(incl, jnp.exp(dec), 0.0)
        parts.append(jnp.dot((cb * seg).astype(BF16), xdtb[:, r * p:(r + 1) * p], preferred_element_type=F32))
    y = y + jnp.concatenate(parts, axis=-1)
    s_scr[...] = s * jnp.exp(cend_x) + _dot_tn(bmb, (xdt * jnp.exp(cend_x - cum_x)).astype(BF16))
    y = (y + d_ref[...] * xs) * jax.nn.silu(z_ref[...])
    y = y * lax.rsqrt(jnp.mean(y * y, axis=-1, keepdims=True) + RMS_EPS)
    o_ref[...] = (y * nw_ref[...]).astype(BF16)

    @pl.when(chunk == pl.num_programs(2) - 1)
    def _():
        for r in range(n_r):
            s_out_ref[0, r] = s_scr[:, r * p:(r + 1) * p]


def _chunk_consts(c):
    tri = jnp.asarray(np.tril(np.ones((c, c), np.float32)), BF16)
    after = jnp.asarray(np.tril(np.ones((c, c), np.float32), -1))
    return tri, after


def _ab_mixers(proj, lay, row0, bsz, seq, conv_g, s_g, conv_s, s_s, conv_gdn_w, gdn_rows, gdn_norm_w,
               conv_ssd_w, conv_ssd_b, ssd_rows, ssd_d_row, ssd_norm_w, expand):
    c = min(CHUNK, seq)
    n_chunks = seq // c
    blk0 = row0 // c
    tri, su = _chunk_consts(c)
    _, _, mask = _halving_levels(c)
    mask = jnp.asarray(mask)
    rows = lambda b, g, ch: blk0 + b * n_chunks + ch
    fixed2 = lambda b, g, ch: (0, 0)
    fixed3 = lambda b, g, ch: (0, 0, 0)
    nk = CONV_W - 1

    h_a, dk, dv, hb = lay['h_a'], lay['dk_a'], lay['dv_a'], lay['hb_a']
    n_hg = h_a // hb
    wk_, wv_ = hb * dk, hb * dv
    qoff, koff, voff = 0, (h_a * dk) // wk_, (2 * h_a * dk) // wv_
    col = lambda width, base: pl.BlockSpec((c, width), lambda b, g, ch: (rows(b, g, ch), base + g))
    cache = lambda width, base: pl.BlockSpec((1, nk, width), lambda b, g, ch: (b, 0, base + g))
    cw = lambda width, base: pl.BlockSpec((CONV_W, width), lambda b, g, ch: (0, base + g))
    st_a = pl.BlockSpec((1, hb, dk, dv), lambda b, g, ch: (b, g, 0, 0))
    alog_a, dtb_a = gdn_rows
    o_a, s_g_new = pl.pallas_call(
        functools.partial(_gdn_body, c=c, hb=hb, dk=dk, dv=dv),
        grid=(bsz, n_hg, n_chunks),
        in_specs=[col(wk_, qoff), col(wk_, koff), col(wv_, voff),
                  col(wv_, lay['za_off'] // wv_),
                  col(V7X_LANES, lay['small_off'] // V7X_LANES),
                  cache(wk_, qoff), cache(wk_, koff), cache(wv_, voff),
                  cw(wk_, qoff), cw(wk_, koff), cw(wv_, voff),
                  pl.BlockSpec((1, 1, V7X_LANES), lambda b, g, ch: (g, 0, 0)),
                  pl.BlockSpec((1, 1, V7X_LANES), lambda b, g, ch: (g, 0, 0)),
                  pl.BlockSpec((1, dv), fixed2),
                  st_a,
                  pl.BlockSpec((c, c), fixed2), pl.BlockSpec((c, c), fixed2),
                  pl.BlockSpec(mask.shape, fixed3)],
        out_specs=[pl.BlockSpec((c, wv_), lambda b, g, ch: (b * n_chunks + ch, g)), st_a],
        out_shape=[jax.ShapeDtypeStruct((bsz * seq, h_a * dv), BF16), jax.ShapeDtypeStruct(s_g.shape, F32)],
        scratch_shapes=[pltpu.VMEM((hb, dk, dv), F32), pltpu.VMEM((8 + c, wk_), F32),
                        pltpu.VMEM((8 + c, wk_), F32), pltpu.VMEM((8 + c, wv_), F32)],
        compiler_params=_params("parallel", "parallel", "arbitrary"),
        name="gdn_mixer",
    )(proj, proj, proj, proj, proj, conv_g, conv_g, conv_g, conv_gdn_w, conv_gdn_w, conv_gdn_w,
      alog_a, dtb_a, gdn_norm_w.reshape(1, dv), s_g.astype(F32), tri, su, mask)

    h_b, p, n_b, d_inner = lay['h_b'], lay['p_b'], lay['n_b'], lay['d_inner']
    n_r = h_b // G_B
    wx = n_r * p
    xoff = lay['xbc_off'] // wx
    boff = (lay['xbc_off'] + d_inner) // n_b
    coff = (lay['xbc_off'] + d_inner + G_B * n_b) // n_b
    cxo, cbo, cco = 0, d_inner // n_b, (d_inner + G_B * n_b) // n_b
    st_b = pl.BlockSpec((1, n_r, n_b, p), lambda b, g, ch: (b, g, 0, 0))
    alog_b, dtb_b = ssd_rows
    bias2 = conv_ssd_b.reshape(1, -1)
    y_b, s_s_new = pl.pallas_call(
        functools.partial(_ssd_body, c=c, n_r=n_r, p=p),
        grid=(bsz, G_B, n_chunks),
        in_specs=[col(wx, xoff), col(n_b, boff), col(n_b, coff),
                  col(wx, lay['zb_off'] // wx),
                  col(V7X_LANES, lay['small_off'] // V7X_LANES + n_hg),
                  cache(wx, cxo), cache(n_b, cbo), cache(n_b, cco),
                  cw(wx, cxo), cw(n_b, cbo), cw(n_b, cco),
                  pl.BlockSpec((1, wx), lambda b, g, ch: (0, cxo + g)),
                  pl.BlockSpec((1, n_b), lambda b, g, ch: (0, cbo + g)),
                  pl.BlockSpec((1, n_b), lambda b, g, ch: (0, cco + g)),
                  pl.BlockSpec((1, 1, V7X_LANES), lambda b, g, ch: (g, 0, 0)),
                  pl.BlockSpec((1, 1, V7X_LANES), lambda b, g, ch: (g, 0, 0)),
                  pl.BlockSpec((1, wx), lambda b, g, ch: (0, g)),
                  pl.BlockSpec((1, wx), lambda b, g, ch: (0, g)),
                  pl.BlockSpec((V7X_LANES, wx), fixed2),
                  st_b,
                  pl.BlockSpec((c, c), fixed2), pl.BlockSpec((c, c), fixed2)],
        out_specs=[pl.BlockSpec((c, wx), lambda b, g, ch: (b * n_chunks + ch, g)), st_b],
        out_shape=[jax.ShapeDtypeStruct((bsz * seq, d_inner), BF16), jax.ShapeDtypeStruct(s_s.shape, F32)],
        scratch_shapes=[pltpu.VMEM((n_b, wx), F32), pltpu.VMEM((8 + c, wx), F32),
                        pltpu.VMEM((8 + c, n_b), F32), pltpu.VMEM((8 + c, n_b), F32)],
        compiler_params=_params("parallel", "parallel", "arbitrary"),
        name="ssd_mixer",
    )(proj, proj, proj, proj, proj, conv_s, conv_s, conv_s, conv_ssd_w, conv_ssd_w, conv_ssd_w,
      bias2, bias2, bias2, alog_b, dtb_b, ssd_d_row, ssd_norm_w.reshape(1, d_inner), expand,
      s_s.astype(F32), tri, su)
    return o_a, s_g_new, y_b, s_s_new


def _split_cols(t, sizes):
    idx = [int(i) for i in np.cumsum(sizes)[:-1]]
    return jnp.split(t, idx, axis=-1)


def kernel(x_prompt, x_sample, cache_conv_gdn, state_gdn, cache_conv_ssd, state_ssd, state_hgrn, w_in_ab, conv_gdn_w, gdn_a_log, gdn_dt_bias, gdn_norm_w, conv_ssd_w, conv_ssd_b, ssd_a_log, ssd_dt_bias, ssd_d, ssd_norm_w, w_out_ab, w_in_c, hgrn_lower_bounds, hgrn_norm_w, w_out_c, router_w, router_b, expert_w_gate_up, expert_b_gate_up, expert_w_down, expert_b_down, ln_mix_g, ln_mix_b, ln_ffn_g, ln_ffn_b):
    bp, lp, d = x_prompt.shape
    bs, ls, _ = x_sample.shape
    depth = ln_mix_g.shape[0]
    n_pairs = depth // 2
    alpha = float((2 * depth) ** 0.25)
    tp, ts = bp * lp, bs * ls
    h_a, dv_a = gdn_a_log.shape[1], gdn_norm_w.shape[1]
    qkv_a = conv_gdn_w.shape[2]
    dk_a = (qkv_a // h_a - dv_a) // 2
    h_b, d_inner = ssd_a_log.shape[1], ssd_norm_w.shape[1]
    p_b = d_inner // h_b
    xbc_b = conv_ssd_w.shape[2]
    n_b = (xbc_b - d_inner) // (2 * G_B)
    hd_c = hgrn_norm_w.shape[1]
    h_c = d // hd_c
    exp_c = hgrn_lower_bounds.shape[1] // h_c
    assert min(lp, ls) >= CONV_W - 1

    hb_a = _pick(h_a, (4, 2, 1))
    n_hg_a = h_a // hb_a
    r_b = h_b // G_B
    lay = dict(h_a=h_a, dk_a=dk_a, dv_a=dv_a, hb_a=hb_a, h_b=h_b, p_b=p_b, n_b=n_b, d_inner=d_inner,
               za_off=qkv_a, zb_off=qkv_a + h_a * dv_a, xbc_off=qkv_a + h_a * dv_a + d_inner,
               small_off=qkv_a + h_a * dv_a + d_inner + xbc_b)
    assert lay['za_off'] % (hb_a * dv_a) == 0 and lay['zb_off'] % (r_b * p_b) == 0
    assert lay['xbc_off'] % (r_b * p_b) == 0 and (lay['xbc_off'] + d_inner) % n_b == 0
    assert lay['small_off'] % V7X_LANES == 0 and d_inner % n_b == 0
    w_qkv, w_b, w_a, w_za, w_zb, w_xbc, w_dt = _split_cols(
        w_in_ab, [qkv_a, h_a, h_a, h_a * dv_a, d_inner, xbc_b, h_b])
    lane_pad = lambda n: jnp.zeros(w_in_ab.shape[:2] + (n,), w_in_ab.dtype)
    small_cols = []
    for gi in range(n_hg_a):
        hs = slice(gi * hb_a, (gi + 1) * hb_a)
        small_cols += [w_b[..., hs], w_a[..., hs], lane_pad(V7X_LANES - 2 * hb_a)]
    for g in range(G_B):
        small_cols += [w_dt[..., g * r_b:(g + 1) * r_b], lane_pad(V7X_LANES - r_b)]
    n_small = (n_hg_a + G_B) * V7X_LANES
    small_cols.append(lane_pad(-n_small % (2 * V7X_MXU_DIM)))
    w_in_ab_r = jnp.concatenate([w_qkv, w_za, w_zb, w_xbc] + small_cols, axis=-1).astype(BF16)
    on_lanes = lambda t, lo: jnp.pad(t.astype(F32), ((0, 0), (0, 0), (lo, V7X_LANES - lo - t.shape[-1])))[:, :, None, :]
    gdn_rows = (on_lanes(gdn_a_log.reshape(n_pairs, n_hg_a, hb_a), hb_a),
                on_lanes(gdn_dt_bias.reshape(n_pairs, n_hg_a, hb_a), hb_a))
    ssd_rows = (on_lanes(ssd_a_log.reshape(n_pairs, G_B, r_b), 0),
                on_lanes(ssd_dt_bias.reshape(n_pairs, G_B, r_b), 0))
    ssd_d_rows = jnp.repeat(ssd_d.astype(F32), p_b, axis=-1)[:, None, :]
    expand_np = np.zeros((V7X_LANES, r_b * p_b), np.float32)
    for r in range(r_b):
        expand_np[r, r * p_b:(r + 1) * p_b] = 1.0
    expand = jnp.asarray(expand_np, BF16)
    w_in_c_b = w_in_c.astype(BF16)
    w_out_ab_b = w_out_ab.astype(BF16)
    w_out_c_b = w_out_c.astype(BF16)
    w_gu_b = expert_w_gate_up.astype(BF16)
    w_d_b = expert_w_down.astype(BF16)

    lb_w = jax.nn.softmax(hgrn_lower_bounds.astype(F32), axis=0)
    lower_bounds = jnp.cumsum(lb_w, axis=0) - lb_w[0]

    x = jnp.concatenate([x_prompt.reshape(tp, d), x_sample.reshape(ts, d)], axis=0)
    xb = x.astype(BF16)
    zeros = lambda *shape: jnp.zeros(shape, F32)
    new = {key: [] for key in ('cg_p', 'sg_p', 'cs_p', 'ss_p', 'sh_p', 'cg_s', 'sg_s', 'cs_s', 'ss_s', 'sh_s')}
    for layer in range(depth):
        j = layer // 2
        if layer % 2 == 0:
            proj = _matmul(xb, w_in_ab_r[j])
            wts = (conv_gdn_w[j], (gdn_rows[0][j], gdn_rows[1][j]), gdn_norm_w[j], conv_ssd_w[j], conv_ssd_b[j],
                   (ssd_rows[0][j], ssd_rows[1][j]), ssd_d_rows[j], ssd_norm_w[j], expand)
            oa_p, sg, yb_p, ss = _ab_mixers(
                proj, lay, 0, bp, lp, zeros(bp, CONV_W - 1, qkv_a), zeros(bp, h_a, dk_a, dv_a),
                zeros(bp, CONV_W - 1, xbc_b), zeros(bp, h_b, n_b, p_b), *wts)
            new['sg_p'].append(sg)
            new['ss_p'].append(ss)
            oa_s, sg, yb_s, ss = _ab_mixers(
                proj, lay, tp, bs, ls, cache_conv_gdn[j], state_gdn[j], cache_conv_ssd[j], state_ssd[j], *wts)
            new['sg_s'].append(sg)
            new['ss_s'].append(ss)
            tail_p = proj[:tp].reshape(bp, lp, -1)[:, lp - (CONV_W - 1):]
            tail_s = proj[tp:].reshape(bs, ls, -1)[:, ls - (CONV_W - 1):]
            xbc_cols = slice(lay['xbc_off'], lay['xbc_off'] + xbc_b)
            new['cg_p'].append(tail_p[..., :qkv_a])
            new['cs_p'].append(tail_p[..., xbc_cols])
            new['cg_s'].append(tail_s[..., :qkv_a])
            new['cs_s'].append(tail_s[..., xbc_cols])
            mixes = [jnp.concatenate([oa_p, oa_s], axis=0), jnp.concatenate([yb_p, yb_s], axis=0)]
            ws = [w_out_ab_b[j, :h_a * dv_a], w_out_ab_b[j, h_a * dv_a:]]
        else:
            proj = _matmul(xb, w_in_c_b[j])
            mix_p, sh = _gla_mixer(proj, 0, bp, lp, lower_bounds[j], zeros(bp, h_c, exp_c, hd_c), hgrn_norm_w[j])
            new['sh_p'].append(sh)
            mix_s, sh = _gla_mixer(proj, tp, bs, ls, lower_bounds[j], state_hgrn[j], hgrn_norm_w[j])
            new['sh_s'].append(sh)
            mixes = [jnp.concatenate([mix_p, mix_s], axis=0)]
            ws = [w_out_c_b[j]]
        x, xb = _outproj_ln(mixes, ws, x, ln_mix_g[layer], ln_mix_b[layer], alpha)
        x, xb = _moe_ln(x, xb, router_w[layer], router_b[layer], w_gu_b[layer], expert_b_gate_up[layer],
                        w_d_b[layer], expert_b_down[layer], ln_ffn_g[layer], ln_ffn_b[layer], alpha)
    st = {key: jnp.stack(val) for key, val in new.items()}
    return (x[:tp].reshape(bp, lp, d), x[tp:].reshape(bs, ls, d),
            st['cg_p'], st['sg_p'], st['cs_p'], st['ss_p'], st['sh_p'],
            st['cg_s'], st['sg_s'], st['cs_s'], st['ss_s'], st['sh_s'])
```

```python
import functools

import jax
import jax.numpy as jnp
import numpy as np
from jax import lax
from jax.experimental import pallas as pl
from jax.experimental.pallas import tpu as pltpu

F32 = jnp.float32
BF16 = jnp.bfloat16

CHUNK = 64
CONV_W = 4
G_B = 2
TOP_K = 4
SWIGLU_LIMIT = 7.0
SWIGLU_ALPHA = 1.702
LN_EPS = 1e-5
RMS_EPS = 1e-6

V7X_LANES = 128
V7X_MXU_DIM = 256
V7X_VMEM_LIMIT_BYTES = 56 * 1024 * 1024

EXPERT_ROWS = 512
EXPERT_FF_COLS = 1024


def _pick(n, candidates):
    for c in candidates:
        if n % c == 0:
            return c
    return n


def _params(*sem):
    return pltpu.CompilerParams(dimension_semantics=sem, vmem_limit_bytes=V7X_VMEM_LIMIT_BYTES)


def _matmul_body(x_ref, w_ref, o_ref):
    o_ref[...] = jnp.dot(x_ref[...], w_ref[...], preferred_element_type=F32)


def _matmul(x, w):
    m, k = x.shape
    n = w.shape[1]
    tm = _pick(m, (768, 512, 256, 128, 64, 32, 16, 8))
    tn = _pick(n, (1024, 512, 256, 128))
    return pl.pallas_call(
        _matmul_body,
        grid=(m // tm, n // tn),
        in_specs=[pl.BlockSpec((tm, k), lambda i, j: (i, 0)),
                  pl.BlockSpec((k, tn), lambda i, j: (0, j))],
        out_specs=pl.BlockSpec((tm, tn), lambda i, j: (i, j)),
        out_shape=jax.ShapeDtypeStruct((m, n), F32),
        compiler_params=_params("parallel", "arbitrary"),
        name="dense_proj",
    )(x, w)


def _layer_norm_rows(y, g, b):
    mu = jnp.mean(y, axis=-1, keepdims=True)
    yc = y - mu
    var = jnp.mean(yc * yc, axis=-1, keepdims=True)
    return yc * lax.rsqrt(var + LN_EPS) * g + b


def _outproj_ln_body(*refs, n_in, alpha):
    mix_refs, w_refs = refs[:n_in], refs[n_in:2 * n_in]
    xres_ref, g_ref, b_ref, of_ref, ob_ref = refs[2 * n_in:]
    h = jnp.dot(mix_refs[0][...], w_refs[0][...], preferred_element_type=F32)
    for mix_ref, w_ref in zip(mix_refs[1:], w_refs[1:]):
        h = h + jnp.dot(mix_ref[...], w_ref[...], preferred_element_type=F32)
    y = _layer_norm_rows(alpha * xres_ref[...] + h, g_ref[...], b_ref[...])
    of_ref[...] = y
    ob_ref[...] = y.astype(BF16)


def _outproj_ln(mixes, ws, xres, g, b, alpha):
    m, d = xres.shape
    tm = _pick(m, (384, 256, 128, 64, 32, 16, 8))
    row = lambda i: (i, 0)
    fixed = lambda i: (0, 0)
    return pl.pallas_call(
        functools.partial(_outproj_ln_body, n_in=len(mixes), alpha=alpha),
        grid=(m // tm,),
        in_specs=([pl.BlockSpec((tm, mix.shape[1]), row) for mix in mixes]
                  + [pl.BlockSpec(w.shape, fixed) for w in ws]
                  + [pl.BlockSpec((tm, d), row), pl.BlockSpec((1, d), fixed), pl.BlockSpec((1, d), fixed)]),
        out_specs=[pl.BlockSpec((tm, d), row), pl.BlockSpec((tm, d), row)],
        out_shape=[jax.ShapeDtypeStruct((m, d), F32), jax.ShapeDtypeStruct((m, d), BF16)],
        compiler_params=_params("parallel"),
        name="outproj_ln",
    )(*mixes, *ws, xres, g.reshape(1, d), b.reshape(1, d))


def _router_body(x_ref, wt_ref, b_ref, idx_ref, gate_ref):
    logits = lax.dot_general(wt_ref[...], x_ref[...], (((1,), (1,)), ((), ())),
                             precision=lax.Precision.HIGHEST, preferred_element_type=F32)
    logits = logits + b_ref[...]
    n_e = logits.shape[0]
    iota = lax.broadcasted_iota(jnp.int32, logits.shape, 0)
    vals, idxs = [], []
    for _ in range(TOP_K):
        m = jnp.max(logits, axis=0, keepdims=True)
        sel = jnp.min(jnp.where(logits == m, iota, n_e), axis=0, keepdims=True)
        vals.append(m)
        idxs.append(sel)
        logits = jnp.where(iota == sel, -jnp.inf, logits)
    exps = [jnp.exp(v - vals[0]) for v in vals]
    denom = exps[0] + exps[1] + exps[2] + exps[3]
    idx_ref[...] = jnp.concatenate(idxs, axis=0)
    gate_ref[...] = jnp.concatenate(exps, axis=0) / denom


def _router(x, router_w, router_b):
    t, d = x.shape
    n_e = router_w.shape[1]
    tb = _pick(t, (768, 512, 256, 128))
    return pl.pallas_call(
        _router_body,
        grid=(t // tb,),
        in_specs=[pl.BlockSpec((tb, d), lambda i: (i, 0)),
                  pl.BlockSpec((n_e, d), lambda i: (0, 0)),
                  pl.BlockSpec((n_e, 1), lambda i: (0, 0))],
        out_specs=[pl.BlockSpec((TOP_K, tb), lambda i: (0, i)),
                   pl.BlockSpec((TOP_K, tb), lambda i: (0, i))],
        out_shape=[jax.ShapeDtypeStruct((TOP_K, t), jnp.int32),
                   jax.ShapeDtypeStruct((TOP_K, t), F32)],
        compiler_params=_params("parallel"),
        name="moe_router",
    )(x, router_w.T, router_b.reshape(n_e, 1))


def _row_copy(src_hbm, row, dst, dst_row, sem):
    return pltpu.make_async_copy(src_hbm.at[pl.ds(row, 1), :], dst.at[pl.ds(dst_row, 1), :], sem)


def _gather_body(nused_ref, tok_cur, tok_nxt, x_hbm, o_ref, buf, sem, *, tg):
    i = pl.program_id(0)
    n_used = nused_ref[0]
    slot = i % 2

    def issue(tok_ref, s):
        def body(r, c):
            _row_copy(x_hbm, tok_ref[r], buf.at[s], r, sem.at[s]).start()
            return c
        lax.fori_loop(0, tg, body, 0)

    @pl.when(jnp.logical_and(i == 0, n_used > 0))
    def _():
        issue(tok_cur, 0)

    @pl.when(i + 1 < n_used)
    def _():
        issue(tok_nxt, 1 - slot)

    @pl.when(i < n_used)
    def _():
        pltpu.make_async_copy(x_hbm.at[pl.ds(0, tg), :], buf.at[slot], sem.at[slot]).wait()
        o_ref[...] = buf[slot].astype(BF16)

    @pl.when(i >= n_used)
    def _():
        o_ref[...] = jnp.zeros_like(o_ref)


def _gather_rows(x, slot_tok, n_used_rows, tg):
    t, d = x.shape
    m_pad = slot_tok.shape[0]
    nblk = m_pad // tg
    n_used = (n_used_rows // tg).astype(jnp.int32).reshape(1)
    last = lambda i, nu: jnp.maximum(jnp.minimum(i, nu[0] - 1), 0)
    grid_spec = pltpu.PrefetchScalarGridSpec(
        num_scalar_prefetch=1,
        grid=(nblk,),
        in_specs=[pl.BlockSpec((tg,), lambda i, nu: (last(i, nu),), memory_space=pltpu.SMEM),
                  pl.BlockSpec((tg,), lambda i, nu: (last(i + 1, nu),), memory_space=pltpu.SMEM),
                  pl.BlockSpec(memory_space=pl.ANY)],
        out_specs=pl.BlockSpec((tg, d), lambda i, nu: (i, 0)),
        scratch_shapes=[pltpu.VMEM((2, tg, d), F32), pltpu.SemaphoreType.DMA((2,))],
    )
    return pl.pallas_call(
        functools.partial(_gather_body, tg=tg),
        grid_spec=grid_spec,
        out_shape=jax.ShapeDtypeStruct((m_pad, d), BF16),
        compiler_params=_params("arbitrary"),
        name="moe_gather",
    )(n_used, slot_tok, slot_tok, x)


def _expert_body(meta_ref, x_ref, wg_ref, wl_ref, wd_ref, bg_ref, bl_ref, bd_ref, o_ref):
    i = pl.program_id(0)
    j = pl.program_id(1)

    @pl.when(i < meta_ref[0])
    def _():
        x = x_ref[...]
        glu = jnp.dot(x, wg_ref[0, 0], preferred_element_type=F32) + bg_ref[0, 0]
        lin = jnp.dot(x, wl_ref[0, 0], preferred_element_type=F32) + bl_ref[0, 0]
        glu = jnp.minimum(glu, SWIGLU_LIMIT)
        lin = jnp.clip(lin, -SWIGLU_LIMIT, SWIGLU_LIMIT)
        act = glu * jax.nn.sigmoid(SWIGLU_ALPHA * glu) * (lin + 1.0)
        part = jnp.dot(act.astype(BF16), wd_ref[0, 0], preferred_element_type=F32)

        @pl.when(j == 0)
        def _():
            o_ref[...] = part + bd_ref[0, 0]

        @pl.when(j > 0)
        def _():
            o_ref[...] += part

    @pl.when(jnp.logical_and(i >= meta_ref[0], j == 0))
    def _():
        o_ref[...] = jnp.zeros_like(o_ref)


def _expert_ffn(xs, block_expert, n_used_blocks, layer, w_gu, b_gu, w_d, b_d, tm):
    m_pad, d = xs.shape
    depth, n_e, _, two_f = w_gu.shape
    d_ff = two_f // 2
    tf = _pick(d_ff, (EXPERT_FF_COLS, 512, 256, 128))
    n_f = d_ff // tf
    nblk = m_pad // tm
    meta = jnp.concatenate([n_used_blocks.reshape(1).astype(jnp.int32), block_expert.astype(jnp.int32)])

    def blk(i, meta):
        return jnp.maximum(jnp.minimum(i, meta[0] - 1), 0)

    def fj(i, j, meta):
        return jnp.where(i < meta[0], j, n_f - 1)

    def expert(i, meta):
        return meta[1 + blk(i, meta)]

    grid_spec = pltpu.PrefetchScalarGridSpec(
        num_scalar_prefetch=1,
        grid=(nblk, n_f),
        in_specs=[
            pl.BlockSpec((tm, d), lambda i, j, m: (blk(i, m), 0)),
            pl.BlockSpec((1, 1, d, tf), lambda i, j, m: (layer, expert(i, m), 0, fj(i, j, m))),
            pl.BlockSpec((1, 1, d, tf), lambda i, j, m: (layer, expert(i, m), 0, n_f + fj(i, j, m))),
            pl.BlockSpec((1, 1, tf, d), lambda i, j, m: (layer, expert(i, m), fj(i, j, m), 0)),
            pl.BlockSpec((1, 1, 1, tf), lambda i, j, m: (layer, expert(i, m), 0, fj(i, j, m))),
            pl.BlockSpec((1, 1, 1, tf), lambda i, j, m: (layer, expert(i, m), 0, n_f + fj(i, j, m))),
            pl.BlockSpec((1, 1, 1, d), lambda i, j, m: (layer, expert(i, m), 0, 0)),
        ],
        out_specs=pl.BlockSpec((tm, d), lambda i, j, m: (i, 0)),
    )
    return pl.pallas_call(
        _expert_body,
        grid_spec=grid_spec,
        out_shape=jax.ShapeDtypeStruct((m_pad, d), F32),
        compiler_params=_params("arbitrary", "arbitrary"),
        name="moe_expert_ffn",
    )(meta, xs, w_gu, w_gu, w_d, b_gu.reshape(depth, n_e, 1, two_f), b_gu.reshape(depth, n_e, 1, two_f),
      b_d.reshape(depth, n_e, 1, d))


def _combine_body(pos_cur, pos_nxt, y_hbm, gate_ref, xres_ref, g_ref, b_ref, of_ref, ob_ref, buf, sem,
                  *, tb, alpha):
    i = pl.program_id(0)
    nblk = pl.num_programs(0)
    slot = i % 2

    def issue(pos_ref, s):
        for k in range(TOP_K):
            def body(t, c, k=k):
                _row_copy(y_hbm, pos_ref[t * TOP_K + k], buf.at[s, k], t, sem.at[s]).start()
                return c
            lax.fori_loop(0, tb, body, 0)

    @pl.when(i == 0)
    def _():
        issue(pos_cur, 0)

    @pl.when(i + 1 < nblk)
    def _():
        issue(pos_nxt, 1 - slot)

    for k in range(TOP_K):
        pltpu.make_async_copy(y_hbm.at[pl.ds(0, tb), :], buf.at[slot, k], sem.at[slot]).wait()
    gate = gate_ref[...]
    f = gate[:, 0:1] * buf[slot, 0]
    for k in range(1, TOP_K):
        f = f + gate[:, k:k + 1] * buf[slot, k]
    y = _layer_norm_rows(alpha * xres_ref[...] + f, g_ref[...], b_ref[...])
    of_ref[...] = y
    ob_ref[...] = y.astype(BF16)


def _combine_ln(y_sorted, pos, gates, xres, g, b, alpha):
    t, d = xres.shape
    tb = _pick(t, (256, 128, 64, 32, 16, 8))
    nblk = t // tb
    row = lambda i: (i, 0)
    fixed = lambda i: (0, 0)
    pos_flat = pos.reshape(t * TOP_K)
    return pl.pallas_call(
        functools.partial(_combine_body, tb=tb, alpha=alpha),
        grid=(nblk,),
        in_specs=[pl.BlockSpec((tb * TOP_K,), lambda i: (i,), memory_space=pltpu.SMEM),
                  pl.BlockSpec((tb * TOP_K,), lambda i: (jnp.minimum(i + 1, nblk - 1),), memory_space=pltpu.SMEM),
                  pl.BlockSpec(memory_space=pl.ANY),
                  pl.BlockSpec((tb, TOP_K), row),
                  pl.BlockSpec((tb, d), row),
                  pl.BlockSpec((1, d), fixed),
                  pl.BlockSpec((1, d), fixed)],
        out_specs=[pl.BlockSpec((tb, d), row), pl.BlockSpec((tb, d), row)],
        out_shape=[jax.ShapeDtypeStruct((t, d), F32), jax.ShapeDtypeStruct((t, d), BF16)],
        scratch_shapes=[pltpu.VMEM((2, TOP_K, tb, d), F32), pltpu.SemaphoreType.DMA((2,))],
        compiler_params=_params("arbitrary"),
        name="moe_combine_ln",
    )(pos_flat, pos_flat, y_sorted, gates, xres, g.reshape(1, d), b.reshape(1, d))


def _moe_ln(x, layer, router_w, router_b, w_gu, b_gu, w_d, b_d, g, b, alpha):
    t, d = x.shape
    n_e = router_w.shape[1]
    m = t * TOP_K
    tm = _pick(m, (EXPERT_ROWS, 512, 256, 128, 64, 32, 16))
    tg = _pick(tm, (256, 128, 64, 32, 16))
    idx_kt, gate_kt = _router(x, router_w, router_b)
    flat_e = idx_kt.T.reshape(m)
    order = jnp.argsort(flat_e, stable=True).astype(jnp.int32)
    e_sorted = flat_e[order]
    counts = jnp.sum(flat_e[:, None] == jnp.arange(n_e, dtype=jnp.int32)[None, :], axis=0, dtype=jnp.int32)
    padded = (counts + tm - 1) // tm * tm
    pad_end = jnp.cumsum(padded)
    pad_start = pad_end - padded
    start = jnp.cumsum(counts) - counts
    dest_sorted = pad_start[e_sorted] + jnp.arange(m, dtype=jnp.int32) - start[e_sorted]
    _, pos = lax.sort_key_val(order, dest_sorted)
    n_blocks = m // tm + n_e
    block_expert = jnp.minimum(
        jnp.searchsorted(pad_end, jnp.arange(n_blocks, dtype=jnp.int32) * tm, side='right'), n_e - 1)
    in_run = (jnp.arange(n_blocks, dtype=jnp.int32) * tm - pad_start[block_expert])[:, None] \
        + jnp.arange(tm, dtype=jnp.int32)[None, :]
    src = jnp.minimum(start[block_expert][:, None] + in_run, m - 1)
    slot_tok = jnp.where(in_run < counts[block_expert][:, None], (order // TOP_K)[src], 0).reshape(n_blocks * tm)
    n_used_rows = pad_end[-1]
    xs = _gather_rows(x, slot_tok, n_used_rows, tg)
    ys = _expert_ffn(xs, block_expert, n_used_rows // tm, layer, w_gu, b_gu, w_d, b_d, tm)
    return _combine_ln(ys, pos.reshape(t, TOP_K), gate_kt.T, x, g, b, alpha)


def _halving_levels(c):
    n_lev = int(np.log2(c))
    assert 2 ** n_lev == c
    idx = np.arange(c)
    upper = np.zeros((n_lev, c, c), np.float32)
    lower = np.zeros((n_lev, c, c), np.float32)
    mask = np.zeros((n_lev, c, c), np.float32)
    for lev in range(n_lev):
        h = 2 ** lev
        blk = idx // (2 * h)
        is_up = (idx % (2 * h)) >= h
        mid = blk * 2 * h + h
        r = idx[None, :]
        upper[lev] = (is_up[:, None] & (r >= mid[:, None]) & (r <= idx[:, None]))
        lower[lev] = ((~is_up)[:, None] & (r > idx[:, None]) & (r <= mid[:, None] - 1))
        mask[lev] = (is_up[:, None] & (~is_up)[None, :] & (blk[:, None] == blk[None, :]))
    return upper, lower, mask


def _split3(x):
    hi = x.astype(BF16)
    r1 = x - hi.astype(F32)
    mid = r1.astype(BF16)
    lo = (r1 - mid.astype(F32)).astype(BF16)
    return hi, mid, lo


def _sel_dot(sel, x):
    hi, mid, lo = _split3(x)
    return (jnp.dot(sel, hi, preferred_element_type=F32) + jnp.dot(sel, mid, preferred_element_type=F32)
            + jnp.dot(sel, lo, preferred_element_type=F32))


def _dot_nt(a, b):
    return lax.dot_general(a, b, (((1,), (1,)), ((), ())), preferred_element_type=F32)


def _dot_tn(a, b):
    return lax.dot_general(a, b, (((0,), (0,)), ((), ())), preferred_element_type=F32)


def _gla_body(q_ref, f_ref, v_ref, g_ref, lb_ref, nw_ref, s0_ref, sums_ref, mask_ref, o_ref, s_out_ref, st_scr,
              *, c, hb, dk, scale):
    n_lev = mask_ref.shape[0]
    chunk = pl.program_id(2)

    @pl.when(chunk == 0)
    def _():
        for h in range(hb):
            st_scr[h] = s0_ref[0, h].T

    sums = sums_ref[...]
    heads = range(hb)
    sls = [slice(h * dk, (h + 1) * dk) for h in heads]
    lbs = [lb_ref[:, sl] for sl in sls]
    sigs = [jax.nn.sigmoid(f_ref[:, sl]) for sl in sls]
    log_f = [jnp.log(lb + (1.0 - lb) * sig) for lb, sig in zip(lbs, sigs)]
    ks = [(1.0 - lb) * (1.0 - sig) for lb, sig in zip(lbs, sigs)]
    qs = [jax.nn.silu(q_ref[:, sl]) * scale for sl in sls]
    vs = [v_ref[:, sl] for sl in sls]
    vbs = [v.astype(BF16) for v in vs]
    cs = [_sel_dot(sums, lf) for lf in log_f]
    sts = [st_scr[h] for h in heads]
    os_ = [_dot_nt((q * jnp.exp(c_[0:c])).astype(BF16), st.astype(BF16)) for q, c_, st in zip(qs, cs, sts)]
    attn = [jnp.zeros((c, c), F32) for _ in heads]
    for lev in range(n_lev):
        up = slice((2 + lev) * c, (3 + lev) * c)
        lo = slice((2 + n_lev + lev) * c, (3 + n_lev + lev) * c)
        qu = [(q * jnp.exp(c_[up])).astype(BF16) for q, c_ in zip(qs, cs)]
        kl = [(k * jnp.exp(c_[lo])).astype(BF16) for k, c_ in zip(ks, cs)]
        attn = [a + mask_ref[lev] * _dot_nt(qu_h, kl_h) for a, qu_h, kl_h in zip(attn, qu, kl)]
    os_ = [o + jnp.dot(a.astype(BF16), vb, preferred_element_type=F32) + jnp.sum(q * k, axis=-1, keepdims=True) * v
           for o, a, vb, q, k, v in zip(os_, attn, vbs, qs, ks, vs)]
    kds = [(k * jnp.exp(c_[c:2 * c])).astype(BF16) for k, c_ in zip(ks, cs)]
    new_st = [st * jnp.exp(c_[c - 1:c, :]) + _dot_tn(vb, kd) for st, c_, vb, kd in zip(sts, cs, vbs, kds)]
    for h in heads:
        st_scr[h] = new_st[h]
        o = os_[h]
        o = o * lax.rsqrt(jnp.mean(o * o, axis=-1, keepdims=True) + RMS_EPS)
        o_ref[:, sls[h]] = (o * nw_ref[...] * jax.nn.silu(g_ref[:, sls[h]])).astype(BF16)

    @pl.when(chunk == pl.num_programs(2) - 1)
    def _():
        for h in range(hb):
            s_out_ref[0, h] = st_scr[h].T


def _gla_mixer(proj, row0, bsz, seq, lb, s0, norm_w):
    _, n_h, dk, dv = s0.shape
    assert dk == dv == V7X_LANES
    c = min(CHUNK, seq)
    n_chunks = seq // c
    hb = _pick(n_h, (8, 4, 2, 1))
    n_hg = n_h // hb
    upper, lower, mask = _halving_levels(c)
    tri = np.tril(np.ones((c, c), np.float32))
    suffix = np.triu(np.ones((c, c), np.float32), 1)
    sums = jnp.asarray(np.concatenate([tri, suffix] + list(upper) + list(lower), axis=0), BF16)
    w = hb * dk
    blk0 = row0 // c
    sec = lambda s: pl.BlockSpec((c, w), lambda b, g, ch: (blk0 + b * n_chunks + ch, s * n_hg + g))
    fixed2 = lambda b, g, ch: (0, 0)
    state_spec = pl.BlockSpec((1, hb, dk, dv), lambda b, g, ch: (b, g, 0, 0))
    out, s_new = pl.pallas_call(
        functools.partial(_gla_body, c=c, hb=hb, dk=dk, scale=float(dk) ** -0.5),
        grid=(bsz, n_hg, n_chunks),
        in_specs=[sec(0), sec(1), sec(2), sec(3),
                  pl.BlockSpec((1, w), lambda b, g, ch: (0, g)),
                  pl.BlockSpec((1, dv), fixed2),
                  state_spec,
                  pl.BlockSpec(sums.shape, fixed2),
                  pl.BlockSpec(mask.shape, lambda b, g, ch: (0, 0, 0))],
        out_specs=[pl.BlockSpec((c, w), lambda b, g, ch: (b * n_chunks + ch, g)), state_spec],
        out_shape=[jax.ShapeDtypeStruct((bsz * seq, n_h * dv), BF16),
                   jax.ShapeDtypeStruct(s0.shape, F32)],
        scratch_shapes=[pltpu.VMEM((hb, dv, dk), F32)],
        compiler_params=_params("parallel", "parallel", "arbitrary"),
        name="gla_mixer",
    )(proj, proj, proj, proj, lb.reshape(1, n_h * dk), norm_w.reshape(1, dv), s0.astype(F32), sums,
      jnp.asarray(mask))
    return out, s_new


def _dot_x3(a, b):
    a_hi = a.astype(BF16)
    b_hi = b.astype(BF16)
    a_lo = (a - a_hi.astype(F32)).astype(BF16)
    b_lo = (b - b_hi.astype(F32)).astype(BF16)
    return (jnp.dot(a_hi, b_hi, preferred_element_type=F32) + jnp.dot(a_hi, b_lo, preferred_element_type=F32)
            + jnp.dot(a_lo, b_hi, preferred_element_type=F32))


def _conv_silu(u_ref, prev_ref, w_ref, bias, cbuf, first, c):
    @pl.when(first)
    def _():
        cbuf[8 - (CONV_W - 1):8, :] = prev_ref[0]

    cbuf[8:8 + c, :] = u_ref[...]
    acc = w_ref[CONV_W - 1:CONV_W, :] * cbuf[8:8 + c, :]
    for j in range(CONV_W - 1):
        off = 8 - (CONV_W - 1) + j
        acc = acc + w_ref[j:j + 1, :] * cbuf[off:off + c, :]
    cbuf[0:8, :] = cbuf[c:c + 8, :]
    if bias is not None:
        acc = acc + bias
    return jax.nn.silu(acc)


def _gdn_body(q_ref, k_ref, v_ref, z_ref, sm_ref, pq_ref, pk_ref, pv_ref, wq_ref, wk_ref, wv_ref,
              alog_ref, dtb_ref, nw_ref, s0_ref, tri_ref, su_ref, mask_ref,
              o_ref, s_out_ref, s_scr, cq, ck, cv, *, c, hb, dk, dv):
    n_lev = mask_ref.shape[0]
    chunk = pl.program_id(2)
    first = chunk == 0

    @pl.when(first)
    def _():
        s_scr[...] = s0_ref[0]

    qs = _conv_silu(q_ref, pq_ref, wq_ref, None, cq, first, c)
    ks = _conv_silu(k_ref, pk_ref, wk_ref, None, ck, first, c)
    vs = _conv_silu(v_ref, pv_ref, wv_ref, None, cv, first, c)
    small = sm_ref[...]
    log_a_all = -jnp.exp(alog_ref[0]) * jax.nn.softplus(small + dtb_ref[0])
    tri = tri_ref[...]
    g_all = _sel_dot(tri, log_a_all)
    row = lax.broadcasted_iota(jnp.int32, (c, c), 0)
    col = lax.broadcasted_iota(jnp.int32, (c, c), 1)
    incl = row >= col
    eye = (row == col).astype(F32)
    heads = range(hb)
    dot = lambda a, b: jnp.dot(a, b, preferred_element_type=F32)
    sl_v = [slice(h * dv, (h + 1) * dv) for h in heads]
    q_ = [qs[:, h * dk:(h + 1) * dk] for h in heads]
    k_ = [ks[:, h * dk:(h + 1) * dk] for h in heads]
    v_ = [vs[:, sl] for sl in sl_v]
    q_ = [q * lax.rsqrt(jnp.sum(q * q, axis=-1, keepdims=True) + 1e-6) * (float(dk) ** -0.5) for q in q_]
    k_ = [k * lax.rsqrt(jnp.sum(k * k, axis=-1, keepdims=True) + 1e-6) for k in k_]
    beta = [jax.nn.sigmoid(small[:, h:h + 1]) for h in heads]
    g_ = [g_all[:, hb + h:hb + h + 1] for h in heads]
    dec = [_sel_dot(tri, log_a_all[:, hb + h:hb + h + 1] * su_ref[...]) for h in heads]
    gamma = [jnp.where(incl, jnp.exp(d), 0.0) for d in dec]
    kb = [k.astype(BF16) for k in k_]
    qb = [q.astype(BF16) for q in q_]
    a_mat = [b * _dot_nt(k, k) * gm for b, k, gm in zip(beta, kb, gamma)]
    qk = [_dot_nt(q, k) * gm for q, k, gm in zip(qb, kb, gamma)]
    t_inv = [eye - mask_ref[0] * a for a in a_mat]
    for lev in range(1, n_lev):
        left = [_dot_x3(t, mask_ref[lev] * a) for t, a in zip(t_inv, a_mat)]
        t_inv = [t - _dot_x3(lf, t) for t, lf in zip(t_inv, left)]
    eg = [jnp.exp(g) for g in g_]
    tb = [t.astype(BF16) for t in t_inv]
    u = [dot(t, (v * b).astype(BF16)) for t, v, b in zip(tb, v_, beta)]
    w = [dot(t, (k * (b * e)).astype(BF16)) for t, k, b, e in zip(tb, k_, beta, eg)]
    s_ = [s_scr[h] for h in heads]
    sb = [s.astype(BF16) for s in s_]
    db = [(u_h - dot(w_h.astype(BF16), s)).astype(BF16) for u_h, w_h, s in zip(u, w, sb)]
    o_ = [dot((q * e).astype(BF16), s) + dot(a.astype(BF16), d) for q, e, s, a, d in zip(q_, eg, sb, qk, db)]
    g_end = [g[c - 1:c, :] for g in g_]
    new_s = [jnp.exp(ge) * s + _dot_tn((k * jnp.exp(ge - g)).astype(BF16), d)
             for ge, s, k, g, d in zip(g_end, s_, k_, g_, db)]
    for h in heads:
        s_scr[h] = new_s[h]
        o = o_[h]
        o = o * lax.rsqrt(jnp.mean(o * o, axis=-1, keepdims=True) + RMS_EPS)
        o_ref[:, sl_v[h]] = (o * nw_ref[...] * jax.nn.silu(z_ref[:, sl_v[h]])).astype(BF16)

    @pl.when(chunk == pl.num_programs(2) - 1)
    def _():
        s_out_ref[0] = s_scr[...]


def _ssd_body(x_ref, b_ref, c_ref, z_ref, sm_ref, px_ref, pb_ref, pc_ref, wx_ref, wb_ref, wc_ref,
              bx_ref, bb_ref, bc_ref, alog_ref, dtb_ref, d_ref, nw_ref, exp_ref, s0_ref, tri_ref, su_ref,
              o_ref, s_out_ref, s_scr, cx, cb_buf, cc_buf, *, c, n_r, p):
    chunk = pl.program_id(2)
    first = chunk == 0

    @pl.when(first)
    def _():
        for r in range(n_r):
            s_scr[:, r * p:(r + 1) * p] = s0_ref[0, r]

    xs = _conv_silu(x_ref, px_ref, wx_ref, bx_ref[...], cx, first, c)
    bm = _conv_silu(b_ref, pb_ref, wb_ref, bb_ref[...], cb_buf, first, c)
    cm = _conv_silu(c_ref, pc_ref, wc_ref, bc_ref[...], cc_buf, first, c)
    dt_all = jax.nn.softplus(sm_ref[...] + dtb_ref[0])
    a_all = -jnp.exp(alog_ref[0]) * dt_all
    tri = tri_ref[...]
    cum_all = _sel_dot(tri, a_all)
    expand = exp_ref[...]
    widen = lambda t: sum(jnp.dot(part, expand, preferred_element_type=F32) for part in _split3(t))
    dt_x = widen(dt_all)
    cum_x = widen(cum_all)
    cend_x = cum_x[c - 1:c, :]
    xdt = xs * dt_x
    row = lax.broadcasted_iota(jnp.int32, (c, c), 0)
    col = lax.broadcasted_iota(jnp.int32, (c, c), 1)
    incl = row >= col
    cmb = cm.astype(BF16)
    bmb = bm.astype(BF16)
    cb = _dot_nt(cmb, bmb)
    s = s_scr[...]
    y = jnp.dot(cmb, s.astype(BF16), preferred_element_type=F32) * jnp.exp(cum_x)
    xdtb = xdt.astype(BF16)
    decs = [_sel_dot(tri, a_all[:, r:r + 1] * su_ref[...]) for r in range(n_r)]
    segs = [jnp.where(incl, jnp.exp(dec), 0.0) for dec in decs]
    parts = [jnp.dot((cb * seg).astype(BF16), xdtb[:, r * p:(r + 1) * p], preferred_element_type=F32)
             for r, seg in enumerate(segs)]
    y = y + jnp.concatenate(parts, axis=-1)
    s_scr[...] = s * jnp.exp(cend_x) + _dot_tn(bmb, (xdt * jnp.exp(cend_x - cum_x)).astype(BF16))
    y = (y + d_ref[...] * xs) * jax.nn.silu(z_ref[...])
    y = y * lax.rsqrt(jnp.mean(y * y, axis=-1, keepdims=True) + RMS_EPS)
    o_ref[...] = (y * nw_ref[...]).astype(BF16)

    @pl.when(chunk == pl.num_programs(2) - 1)
    def _():
        for r in range(n_r):
            s_out_ref[0, r] = s_scr[:, r * p:(r + 1) * p]


def _chunk_consts(c):
    tri = jnp.asarray(np.tril(np.ones((c, c), np.float32)), BF16)
    after = jnp.asarray(np.tril(np.ones((c, c), np.float32), -1))
    return tri, after


def _ab_mixers(proj, lay, row0, bsz, seq, conv_g, s_g, conv_s, s_s, conv_gdn_w, gdn_rows, gdn_norm_w,
               conv_ssd_w, conv_ssd_b, ssd_rows, ssd_d_row, ssd_norm_w, expand):
    c = min(CHUNK, seq)
    n_chunks = seq // c
    blk0 = row0 // c
    tri, su = _chunk_consts(c)
    _, _, mask = _halving_levels(c)
    mask = jnp.asarray(mask)
    rows = lambda b, g, ch: blk0 + b * n_chunks + ch
    fixed2 = lambda b, g, ch: (0, 0)
    fixed3 = lambda b, g, ch: (0, 0, 0)
    nk = CONV_W - 1

    h_a, dk, dv, hb = lay['h_a'], lay['dk_a'], lay['dv_a'], lay['hb_a']
    n_hg = h_a // hb
    wk_, wv_ = hb * dk, hb * dv
    qoff, koff, voff = 0, (h_a * dk) // wk_, (2 * h_a * dk) // wv_
    col = lambda width, base: pl.BlockSpec((c, width), lambda b, g, ch: (rows(b, g, ch), base + g))
    cache = lambda width, base: pl.BlockSpec((1, nk, width), lambda b, g, ch: (b, 0, base + g))
    cw = lambda width, base: pl.BlockSpec((CONV_W, width), lambda b, g, ch: (0, base + g))
    st_a = pl.BlockSpec((1, hb, dk, dv), lambda b, g, ch: (b, g, 0, 0))
    alog_a, dtb_a = gdn_rows
    o_a, s_g_new = pl.pallas_call(
        functools.partial(_gdn_body, c=c, hb=hb, dk=dk, dv=dv),
        grid=(bsz, n_hg, n_chunks),
        in_specs=[col(wk_, qoff), col(wk_, koff), col(wv_, voff),
                  col(wv_, lay['za_off'] // wv_),
                  col(V7X_LANES, lay['small_off'] // V7X_LANES),
                  cache(wk_, qoff), cache(wk_, koff), cache(wv_, voff),
                  cw(wk_, qoff), cw(wk_, koff), cw(wv_, voff),
                  pl.BlockSpec((1, 1, V7X_LANES), lambda b, g, ch: (g, 0, 0)),
                  pl.BlockSpec((1, 1, V7X_LANES), lambda b, g, ch: (g, 0, 0)),
                  pl.BlockSpec((1, dv), fixed2),
                  st_a,
                  pl.BlockSpec((c, c), fixed2), pl.BlockSpec((c, c), fixed2),
                  pl.BlockSpec(mask.shape, fixed3)],
        out_specs=[pl.BlockSpec((c, wv_), lambda b, g, ch: (b * n_chunks + ch, g)), st_a],
        out_shape=[jax.ShapeDtypeStruct((bsz * seq, h_a * dv), BF16), jax.ShapeDtypeStruct(s_g.shape, F32)],
        scratch_shapes=[pltpu.VMEM((hb, dk, dv), F32), pltpu.VMEM((8 + c, wk_), F32),
                        pltpu.VMEM((8 + c, wk_), F32), pltpu.VMEM((8 + c, wv_), F32)],
        compiler_params=_params("parallel", "parallel", "arbitrary"),
        name="gdn_mixer",
    )(proj, proj, proj, proj, proj, conv_g, conv_g, conv_g, conv_gdn_w, conv_gdn_w, conv_gdn_w,
      alog_a, dtb_a, gdn_norm_w.reshape(1, dv), s_g.astype(F32), tri, su, mask)

    h_b, p, n_b, d_inner = lay['h_b'], lay['p_b'], lay['n_b'], lay['d_inner']
    n_r = h_b // G_B
    wx = n_r * p
    xoff = lay['xbc_off'] // wx
    boff = (lay['xbc_off'] + d_inner) // n_b
    coff = (lay['xbc_off'] + d_inner + G_B * n_b) // n_b
    cxo, cbo, cco = 0, d_inner // n_b, (d_inner + G_B * n_b) // n_b
    st_b = pl.BlockSpec((1, n_r, n_b, p), lambda b, g, ch: (b, g, 0, 0))
    alog_b, dtb_b = ssd_rows
    bias2 = conv_ssd_b.reshape(1, -1)
    y_b, s_s_new = pl.pallas_call(
        functools.partial(_ssd_body, c=c, n_r=n_r, p=p),
        grid=(bsz, G_B, n_chunks),
        in_specs=[col(wx, xoff), col(n_b, boff), col(n_b, coff),
                  col(wx, lay['zb_off'] // wx),
                  col(V7X_LANES, lay['small_off'] // V7X_LANES + n_hg),
                  cache(wx, cxo), cache(n_b, cbo), cache(n_b, cco),
                  cw(wx, cxo), cw(n_b, cbo), cw(n_b, cco),
                  pl.BlockSpec((1, wx), lambda b, g, ch: (0, cxo + g)),
                  pl.BlockSpec((1, n_b), lambda b, g, ch: (0, cbo + g)),
                  pl.BlockSpec((1, n_b), lambda b, g, ch: (0, cco + g)),
                  pl.BlockSpec((1, 1, V7X_LANES), lambda b, g, ch: (g, 0, 0)),
                  pl.BlockSpec((1, 1, V7X_LANES), lambda b, g, ch: (g, 0, 0)),
                  pl.BlockSpec((1, wx), lambda b, g, ch: (0, g)),
                  pl.BlockSpec((1, wx), lambda b, g, ch: (0, g)),
                  pl.BlockSpec((V7X_LANES, wx), fixed2),
                  st_b,
                  pl.BlockSpec((c, c), fixed2), pl.BlockSpec((c, c), fixed2)],
        out_specs=[pl.BlockSpec((c, wx), lambda b, g, ch: (b * n_chunks + ch, g)), st_b],
        out_shape=[jax.ShapeDtypeStruct((bsz * seq, d_inner), BF16), jax.ShapeDtypeStruct(s_s.shape, F32)],
        scratch_shapes=[pltpu.VMEM((n_b, wx), F32), pltpu.VMEM((8 + c, wx), F32),
                        pltpu.VMEM((8 + c, n_b), F32), pltpu.VMEM((8 + c, n_b), F32)],
        compiler_params=_params("parallel", "parallel", "arbitrary"),
        name="ssd_mixer",
    )(proj, proj, proj, proj, proj, conv_s, conv_s, conv_s, conv_ssd_w, conv_ssd_w, conv_ssd_w,
      bias2, bias2, bias2, alog_b, dtb_b, ssd_d_row, ssd_norm_w.reshape(1, d_inner), expand,
      s_s.astype(F32), tri, su)
    return o_a, s_g_new, y_b, s_s_new


def _split_cols(t, sizes):
    idx = [int(i) for i in np.cumsum(sizes)[:-1]]
    return jnp.split(t, idx, axis=-1)


def kernel(x_prompt, x_sample, cache_conv_gdn, state_gdn, cache_conv_ssd, state_ssd, state_hgrn, w_in_ab, conv_gdn_w, gdn_a_log, gdn_dt_bias, gdn_norm_w, conv_ssd_w, conv_ssd_b, ssd_a_log, ssd_dt_bias, ssd_d, ssd_norm_w, w_out_ab, w_in_c, hgrn_lower_bounds, hgrn_norm_w, w_out_c, router_w, router_b, expert_w_gate_up, expert_b_gate_up, expert_w_down, expert_b_down, ln_mix_g, ln_mix_b, ln_ffn_g, ln_ffn_b):
    bp, lp, d = x_prompt.shape
    bs, ls, _ = x_sample.shape
    depth = ln_mix_g.shape[0]
    n_pairs = depth // 2
    alpha = float((2 * depth) ** 0.25)
    tp, ts = bp * lp, bs * ls
    h_a, dv_a = gdn_a_log.shape[1], gdn_norm_w.shape[1]
    qkv_a = conv_gdn_w.shape[2]
    dk_a = (qkv_a // h_a - dv_a) // 2
    h_b, d_inner = ssd_a_log.shape[1], ssd_norm_w.shape[1]
    p_b = d_inner // h_b
    xbc_b = conv_ssd_w.shape[2]
    n_b = (xbc_b - d_inner) // (2 * G_B)
    hd_c = hgrn_norm_w.shape[1]
    h_c = d // hd_c
    exp_c = hgrn_lower_bounds.shape[1] // h_c
    assert min(lp, ls) >= CONV_W - 1

    hb_a = _pick(h_a, (8, 4, 2, 1))
    n_hg_a = h_a // hb_a
    r_b = h_b // G_B
    lay = dict(h_a=h_a, dk_a=dk_a, dv_a=dv_a, hb_a=hb_a, h_b=h_b, p_b=p_b, n_b=n_b, d_inner=d_inner,
               za_off=qkv_a, zb_off=qkv_a + h_a * dv_a, xbc_off=qkv_a + h_a * dv_a + d_inner,
               small_off=qkv_a + h_a * dv_a + d_inner + xbc_b)
    assert lay['za_off'] % (hb_a * dv_a) == 0 and lay['zb_off'] % (r_b * p_b) == 0
    assert lay['xbc_off'] % (r_b * p_b) == 0 and (lay['xbc_off'] + d_inner) % n_b == 0
    assert lay['small_off'] % V7X_LANES == 0 and d_inner % n_b == 0
    w_qkv, w_b, w_a, w_za, w_zb, w_xbc, w_dt = _split_cols(
        w_in_ab, [qkv_a, h_a, h_a, h_a * dv_a, d_inner, xbc_b, h_b])
    lane_pad = lambda n: jnp.zeros(w_in_ab.shape[:2] + (n,), w_in_ab.dtype)
    small_cols = []
    for gi in range(n_hg_a):
        hs = slice(gi * hb_a, (gi + 1) * hb_a)
        small_cols += [w_b[..., hs], w_a[..., hs], lane_pad(V7X_LANES - 2 * hb_a)]
    for g in range(G_B):
        small_cols += [w_dt[..., g * r_b:(g + 1) * r_b], lane_pad(V7X_LANES - r_b)]
    n_small = (n_hg_a + G_B) * V7X_LANES
    small_cols.append(lane_pad(-n_small % (2 * V7X_MXU_DIM)))
    w_in_ab_r = jnp.concatenate([w_qkv, w_za, w_zb, w_xbc] + small_cols, axis=-1).astype(BF16)
    on_lanes = lambda t, lo: jnp.pad(t.astype(F32), ((0, 0), (0, 0), (lo, V7X_LANES - lo - t.shape[-1])))[:, :, None, :]
    gdn_rows = (on_lanes(gdn_a_log.reshape(n_pairs, n_hg_a, hb_a), hb_a),
                on_lanes(gdn_dt_bias.reshape(n_pairs, n_hg_a, hb_a), hb_a))
    ssd_rows = (on_lanes(ssd_a_log.reshape(n_pairs, G_B, r_b), 0),
                on_lanes(ssd_dt_bias.reshape(n_pairs, G_B, r_b), 0))
    ssd_d_rows = jnp.repeat(ssd_d.astype(F32), p_b, axis=-1)[:, None, :]
    expand_np = np.zeros((V7X_LANES, r_b * p_b), np.float32)
    for r in range(r_b):
        expand_np[r, r * p_b:(r + 1) * p_b] = 1.0
    expand = jnp.asarray(expand_np, BF16)
    w_in_c_b = w_in_c.astype(BF16)
    w_out_ab_b = w_out_ab.astype(BF16)
    w_out_c_b = w_out_c.astype(BF16)
    w_gu_b = expert_w_gate_up.astype(BF16)
    w_d_b = expert_w_down.astype(BF16)

    lb_w = jax.nn.softmax(hgrn_lower_bounds.astype(F32), axis=0)
    lower_bounds = jnp.cumsum(lb_w, axis=0) - lb_w[0]

    x = jnp.concatenate([x_prompt.reshape(tp, d), x_sample.reshape(ts, d)], axis=0)
    xb = x.astype(BF16)
    zeros = lambda *shape: jnp.zeros(shape, F32)
    new = {key: [] for key in ('cg_p', 'sg_p', 'cs_p', 'ss_p', 'sh_p', 'cg_s', 'sg_s', 'cs_s', 'ss_s', 'sh_s')}
    for layer in range(depth):
        j = layer // 2
        if layer % 2 == 0:
            proj = _matmul(xb, w_in_ab_r[j])
            wts = (conv_gdn_w[j], (gdn_rows[0][j], gdn_rows[1][j]), gdn_norm_w[j], conv_ssd_w[j], conv_ssd_b[j],
                   (ssd_rows[0][j], ssd_rows[1][j]), ssd_d_rows[j], ssd_norm_w[j], expand)
            oa_p, sg, yb_p, ss = _ab_mixers(
                proj, lay, 0, bp, lp, zeros(bp, CONV_W - 1, qkv_a), zeros(bp, h_a, dk_a, dv_a),
                zeros(bp, CONV_W - 1, xbc_b), zeros(bp, h_b, n_b, p_b), *wts)
            new['sg_p'].append(sg)
            new['ss_p'].append(ss)
            oa_s, sg, yb_s, ss = _ab_mixers(
                proj, lay, tp, bs, ls, cache_conv_gdn[j], state_gdn[j], cache_conv_ssd[j], state_ssd[j], *wts)
            new['sg_s'].append(sg)
            new['ss_s'].append(ss)
            nk = CONV_W - 1
            tail_p = jnp.stack([proj[(b + 1) * lp - nk:(b + 1) * lp] for b in range(bp)])
            tail_s = jnp.stack([proj[tp + (b + 1) * ls - nk:tp + (b + 1) * ls] for b in range(bs)])
            xbc_cols = slice(lay['xbc_off'], lay['xbc_off'] + xbc_b)
            new['cg_p'].append(tail_p[..., :qkv_a])
            new['cs_p'].append(tail_p[..., xbc_cols])
            new['cg_s'].append(tail_s[..., :qkv_a])
            new['cs_s'].append(tail_s[..., xbc_cols])
            mixes = [jnp.concatenate([oa_p, oa_s], axis=0), jnp.concatenate([yb_p, yb_s], axis=0)]
            ws = [w_out_ab_b[j, :h_a * dv_a], w_out_ab_b[j, h_a * dv_a:]]
        else:
            proj = _matmul(xb, w_in_c_b[j])
            mix_p, sh = _gla_mixer(proj, 0, bp, lp, lower_bounds[j], zeros(bp, h_c, exp_c, hd_c), hgrn_norm_w[j])
            new['sh_p'].append(sh)
            mix_s, sh = _gla_mixer(proj, tp, bs, ls, lower_bounds[j], state_hgrn[j], hgrn_norm_w[j])
            new['sh_s'].append(sh)
            mixes = [jnp.concatenate([mix_p, mix_s], axis=0)]
            ws = [w_out_c_b[j]]
        x, xb = _outproj_ln(mixes, ws, x, ln_mix_g[layer], ln_mix_b[layer], alpha)
        x, xb = _moe_ln(x, layer, router_w[layer], router_b[layer], w_gu_b, expert_b_gate_up,
                        w_d_b, expert_b_down, ln_ffn_g[layer], ln_ffn_b[layer], alpha)
    st = {key: jnp.stack(val) for key, val in new.items()}
    return (x[:tp].reshape(bp, lp, d), x[tp:].reshape(bs, ls, d),
            st['cg_p'], st['sg_p'], st['cs_p'], st['ss_p'], st['sh_p'],
            st['cg_s'], st['sg_s'], st['cs_s'], st['ss_s'], st['sh_s'])
```

```python
import functools

import jax
import jax.numpy as jnp
import numpy as np
from jax import lax
from jax.experimental import pallas as pl
from jax.experimental.pallas import tpu as pltpu

F32 = jnp.float32
BF16 = jnp.bfloat16

CHUNK = 64
CONV_W = 4
G_B = 2
TOP_K = 4
SWIGLU_LIMIT = 7.0
SWIGLU_ALPHA = 1.702
LN_EPS = 1e-5
RMS_EPS = 1e-6

V7X_LANES = 128
V7X_MXU_DIM = 256
V7X_VMEM_LIMIT_BYTES = 56 * 1024 * 1024

EXPERT_ROWS = 512
EXPERT_FF_COLS = 1024
ROW_DMA_UNROLL = 8


def _pick(n, candidates):
    for c in candidates:
        if n % c == 0:
            return c
    return n


def _params(*sem):
    return pltpu.CompilerParams(dimension_semantics=sem, vmem_limit_bytes=V7X_VMEM_LIMIT_BYTES)


def _matmul_body(x_ref, w_ref, o_ref):
    o_ref[...] = jnp.dot(x_ref[...], w_ref[...], preferred_element_type=F32)


def _matmul(x, w):
    m, k = x.shape
    n = w.shape[1]
    tm = _pick(m, (768, 512, 256, 128, 64, 32, 16, 8))
    tn = _pick(n, (1024, 512, 256, 128))
    return pl.pallas_call(
        _matmul_body,
        grid=(m // tm, n // tn),
        in_specs=[pl.BlockSpec((tm, k), lambda i, j: (i, 0)),
                  pl.BlockSpec((k, tn), lambda i, j: (0, j))],
        out_specs=pl.BlockSpec((tm, tn), lambda i, j: (i, j)),
        out_shape=jax.ShapeDtypeStruct((m, n), F32),
        compiler_params=_params("parallel", "arbitrary"),
        name="dense_proj",
    )(x, w)


def _layer_norm_rows(y, g, b):
    mu = jnp.mean(y, axis=-1, keepdims=True)
    yc = y - mu
    var = jnp.mean(yc * yc, axis=-1, keepdims=True)
    return yc * lax.rsqrt(var + LN_EPS) * g + b


def _pack_halves(y):
    half = y.shape[-1] // 2
    bits = lambda v: lax.bitcast_convert_type(v.astype(BF16).astype(F32), jnp.uint32)
    return (bits(y[:, :half]) >> 16) | bits(y[:, half:])


def _unpack_halves(p):
    lo = lax.bitcast_convert_type(p << 16, F32)
    hi = lax.bitcast_convert_type(p & jnp.uint32(0xFFFF0000), F32)
    return lo, hi


def _outproj_ln_body(*refs, n_in, alpha):
    mix_refs, w_refs = refs[:n_in], refs[n_in:2 * n_in]
    xres_ref, g_ref, b_ref, of_ref, op_ref = refs[2 * n_in:]
    h = jnp.dot(mix_refs[0][...], w_refs[0][...], preferred_element_type=F32)
    for mix_ref, w_ref in zip(mix_refs[1:], w_refs[1:]):
        h = h + jnp.dot(mix_ref[...], w_ref[...], preferred_element_type=F32)
    y = _layer_norm_rows(alpha * xres_ref[...] + h, g_ref[...], b_ref[...])
    of_ref[...] = y
    op_ref[...] = _pack_halves(y)


def _outproj_ln(mixes, ws, xres, g, b, alpha):
    m, d = xres.shape
    tm = _pick(m, (384, 256, 128, 64, 32, 16, 8))
    row = lambda i: (i, 0)
    fixed = lambda i: (0, 0)
    return pl.pallas_call(
        functools.partial(_outproj_ln_body, n_in=len(mixes), alpha=alpha),
        grid=(m // tm,),
        in_specs=([pl.BlockSpec((tm, mix.shape[1]), row) for mix in mixes]
                  + [pl.BlockSpec(w.shape, fixed) for w in ws]
                  + [pl.BlockSpec((tm, d), row), pl.BlockSpec((1, d), fixed), pl.BlockSpec((1, d), fixed)]),
        out_specs=[pl.BlockSpec((tm, d), row), pl.BlockSpec((tm, d // 2), row)],
        out_shape=[jax.ShapeDtypeStruct((m, d), F32), jax.ShapeDtypeStruct((m, d // 2), jnp.uint32)],
        compiler_params=_params("parallel"),
        name="outproj_ln",
    )(*mixes, *ws, xres, g.reshape(1, d), b.reshape(1, d))


def _router_body(x_ref, wt_ref, b_ref, idx_ref, gate_ref):
    logits = lax.dot_general(wt_ref[...], x_ref[...], (((1,), (1,)), ((), ())),
                             precision=lax.Precision.HIGHEST, preferred_element_type=F32)
    logits = logits + b_ref[...]
    n_e = logits.shape[0]
    iota = lax.broadcasted_iota(jnp.int32, logits.shape, 0)
    vals, idxs = [], []
    for _ in range(TOP_K):
        m = jnp.max(logits, axis=0, keepdims=True)
        sel = jnp.min(jnp.where(logits == m, iota, n_e), axis=0, keepdims=True)
        vals.append(m)
        idxs.append(sel)
        logits = jnp.where(iota == sel, -jnp.inf, logits)
    exps = [jnp.exp(v - vals[0]) for v in vals]
    denom = exps[0] + exps[1] + exps[2] + exps[3]
    idx_ref[...] = jnp.concatenate(idxs, axis=0)
    gate_ref[...] = jnp.concatenate(exps, axis=0) / denom


def _router(x, router_w, router_b):
    t, d = x.shape
    n_e = router_w.shape[1]
    tb = _pick(t, (768, 512, 256, 128))
    return pl.pallas_call(
        _router_body,
        grid=(t // tb,),
        in_specs=[pl.BlockSpec((tb, d), lambda i: (i, 0)),
                  pl.BlockSpec((n_e, d), lambda i: (0, 0)),
                  pl.BlockSpec((n_e, 1), lambda i: (0, 0))],
        out_specs=[pl.BlockSpec((TOP_K, tb), lambda i: (0, i)),
                   pl.BlockSpec((TOP_K, tb), lambda i: (0, i))],
        out_shape=[jax.ShapeDtypeStruct((TOP_K, t), jnp.int32),
                   jax.ShapeDtypeStruct((TOP_K, t), F32)],
        compiler_params=_params("parallel"),
        name="moe_router",
    )(x, router_w.T, router_b.reshape(n_e, 1))


def _row_copy(src_hbm, row, dst, dst_row, sem):
    return pltpu.make_async_copy(src_hbm.at[pl.ds(row, 1), :], dst.at[pl.ds(dst_row, 1), :], sem)


def _gather_body(nused_ref, tok_cur, tok_nxt, x_hbm, o_ref, buf, sem, *, tg):
    i = pl.program_id(0)
    n_used = nused_ref[0]
    slot = i % 2

    def issue(tok_ref, s):
        def body(r, c):
            _row_copy(x_hbm, tok_ref[r], buf.at[s], r, sem.at[s]).start()
            return c
        lax.fori_loop(0, tg, body, 0, unroll=ROW_DMA_UNROLL)

    @pl.when(jnp.logical_and(i == 0, n_used > 0))
    def _():
        issue(tok_cur, 0)

    @pl.when(i + 1 < n_used)
    def _():
        issue(tok_nxt, 1 - slot)

    @pl.when(i < n_used)
    def _():
        pltpu.make_async_copy(x_hbm.at[pl.ds(0, tg), :], buf.at[slot], sem.at[slot]).wait()
        lo, hi = _unpack_halves(buf[slot])
        half = lo.shape[-1]
        o_ref[:, :half] = lo.astype(BF16)
        o_ref[:, half:] = hi.astype(BF16)

    @pl.when(i >= n_used)
    def _():
        o_ref[...] = jnp.zeros_like(o_ref)


def _gather_rows(xp, slot_tok, n_used_rows, tg):
    t, half = xp.shape
    d = 2 * half
    m_pad = slot_tok.shape[0]
    nblk = m_pad // tg
    n_used = (n_used_rows // tg).astype(jnp.int32).reshape(1)
    last = lambda i, nu: jnp.maximum(jnp.minimum(i, nu[0] - 1), 0)
    grid_spec = pltpu.PrefetchScalarGridSpec(
        num_scalar_prefetch=1,
        grid=(nblk,),
        in_specs=[pl.BlockSpec((tg,), lambda i, nu: (last(i, nu),), memory_space=pltpu.SMEM),
                  pl.BlockSpec((tg,), lambda i, nu: (last(i + 1, nu),), memory_space=pltpu.SMEM),
                  pl.BlockSpec(memory_space=pl.ANY)],
        out_specs=pl.BlockSpec((tg, d), lambda i, nu: (i, 0)),
        scratch_shapes=[pltpu.VMEM((2, tg, half), jnp.uint32), pltpu.SemaphoreType.DMA((2,))],
    )
    return pl.pallas_call(
        functools.partial(_gather_body, tg=tg),
        grid_spec=grid_spec,
        out_shape=jax.ShapeDtypeStruct((m_pad, d), BF16),
        compiler_params=_params("arbitrary"),
        name="moe_gather",
    )(n_used, slot_tok, slot_tok, xp)


def _expert_body(meta_ref, x_ref, wg_ref, wl_ref, wd_ref, bg_ref, bl_ref, bd_ref, o_ref, acc_ref):
    i = pl.program_id(0)
    j = pl.program_id(1)

    @pl.when(i < meta_ref[0])
    def _():
        x = x_ref[...]
        glu = jnp.dot(x, wg_ref[0, 0], preferred_element_type=F32) + bg_ref[0, 0]
        lin = jnp.dot(x, wl_ref[0, 0], preferred_element_type=F32) + bl_ref[0, 0]
        glu = jnp.minimum(glu, SWIGLU_LIMIT)
        lin = jnp.clip(lin, -SWIGLU_LIMIT, SWIGLU_LIMIT)
        act = glu * jax.nn.sigmoid(SWIGLU_ALPHA * glu) * (lin + 1.0)
        part = jnp.dot(act.astype(BF16), wd_ref[0, 0], preferred_element_type=F32)

        @pl.when(j == 0)
        def _():
            acc_ref[...] = part + bd_ref[0, 0]

        @pl.when(j > 0)
        def _():
            acc_ref[...] += part

        @pl.when(j == pl.num_programs(1) - 1)
        def _():
            o_ref[...] = _pack_halves(acc_ref[...])

    @pl.when(jnp.logical_and(i >= meta_ref[0], j == 0))
    def _():
        o_ref[...] = jnp.zeros_like(o_ref)


def _expert_ffn(xs, block_expert, n_used_blocks, layer, w_gu, b_gu, w_d, b_d, tm):
    m_pad, d = xs.shape
    depth, n_e, _, two_f = w_gu.shape
    d_ff = two_f // 2
    tf = _pick(d_ff, (EXPERT_FF_COLS, 512, 256, 128))
    n_f = d_ff // tf
    nblk = m_pad // tm
    meta = jnp.concatenate([n_used_blocks.reshape(1).astype(jnp.int32), block_expert.astype(jnp.int32)])

    def blk(i, meta):
        return jnp.maximum(jnp.minimum(i, meta[0] - 1), 0)

    def fj(i, j, meta):
        return jnp.where(i < meta[0], j, n_f - 1)

    def expert(i, meta):
        return meta[1 + blk(i, meta)]

    grid_spec = pltpu.PrefetchScalarGridSpec(
        num_scalar_prefetch=1,
        grid=(nblk, n_f),
        in_specs=[
            pl.BlockSpec((tm, d), lambda i, j, m: (blk(i, m), 0)),
            pl.BlockSpec((1, 1, d, tf), lambda i, j, m: (layer, expert(i, m), 0, fj(i, j, m))),
            pl.BlockSpec((1, 1, d, tf), lambda i, j, m: (layer, expert(i, m), 0, n_f + fj(i, j, m))),
            pl.BlockSpec((1, 1, tf, d), lambda i, j, m: (layer, expert(i, m), fj(i, j, m), 0)),
            pl.BlockSpec((1, 1, 1, tf), lambda i, j, m: (layer, expert(i, m), 0, fj(i, j, m))),
            pl.BlockSpec((1, 1, 1, tf), lambda i, j, m: (layer, expert(i, m), 0, n_f + fj(i, j, m))),
            pl.BlockSpec((1, 1, 1, d), lambda i, j, m: (layer, expert(i, m), 0, 0)),
        ],
        out_specs=pl.BlockSpec((tm, d // 2), lambda i, j, m: (i, 0)),
        scratch_shapes=[pltpu.VMEM((tm, d), F32)],
    )
    return pl.pallas_call(
        _expert_body,
        grid_spec=grid_spec,
        out_shape=jax.ShapeDtypeStruct((m_pad, d // 2), jnp.uint32),
        compiler_params=_params("arbitrary", "arbitrary"),
        name="moe_expert_ffn",
    )(meta, xs, w_gu, w_gu, w_d, b_gu.reshape(depth, n_e, 1, two_f), b_gu.reshape(depth, n_e, 1, two_f),
      b_d.reshape(depth, n_e, 1, d))


def _combine_body(pos_cur, pos_nxt, y_hbm, gate_ref, xres_ref, g_ref, b_ref, of_ref, ob_ref, buf, sem,
                  *, tb, alpha):
    i = pl.program_id(0)
    nblk = pl.num_programs(0)
    slot = i % 2

    def issue(pos_ref, s):
        for k in range(TOP_K):
            def body(t, c, k=k):
                _row_copy(y_hbm, pos_ref[t * TOP_K + k], buf.at[s, k], t, sem.at[s]).start()
                return c
            lax.fori_loop(0, tb, body, 0, unroll=ROW_DMA_UNROLL)

    @pl.when(i == 0)
    def _():
        issue(pos_cur, 0)

    @pl.when(i + 1 < nblk)
    def _():
        issue(pos_nxt, 1 - slot)

    for k in range(TOP_K):
        pltpu.make_async_copy(y_hbm.at[pl.ds(0, tb), :], buf.at[slot, k], sem.at[slot]).wait()
    gate = gate_ref[...]
    f_lo = f_hi = None
    for k in range(TOP_K):
        lo, hi = _unpack_halves(buf[slot, k])
        gk = gate[:, k:k + 1]
        f_lo = gk * lo if f_lo is None else f_lo + gk * lo
        f_hi = gk * hi if f_hi is None else f_hi + gk * hi
    f = jnp.concatenate([f_lo, f_hi], axis=-1)
    y = _layer_norm_rows(alpha * xres_ref[...] + f, g_ref[...], b_ref[...])
    of_ref[...] = y
    ob_ref[...] = y.astype(BF16)


def _combine_ln(y_sorted, pos, gates, xres, g, b, alpha):
    t, d = xres.shape
    tb = _pick(t, (256, 128, 64, 32, 16, 8))
    nblk = t // tb
    row = lambda i: (i, 0)
    fixed = lambda i: (0, 0)
    pos_flat = pos.reshape(t * TOP_K)
    return pl.pallas_call(
        functools.partial(_combine_body, tb=tb, alpha=alpha),
        grid=(nblk,),
        in_specs=[pl.BlockSpec((tb * TOP_K,), lambda i: (i,), memory_space=pltpu.SMEM),
                  pl.BlockSpec((tb * TOP_K,), lambda i: (jnp.minimum(i + 1, nblk - 1),), memory_space=pltpu.SMEM),
                  pl.BlockSpec(memory_space=pl.ANY),
                  pl.BlockSpec((tb, TOP_K), row),
                  pl.BlockSpec((tb, d), row),
                  pl.BlockSpec((1, d), fixed),
                  pl.BlockSpec((1, d), fixed)],
        out_specs=[pl.BlockSpec((tb, d), row), pl.BlockSpec((tb, d), row)],
        out_shape=[jax.ShapeDtypeStruct((t, d), F32), jax.ShapeDtypeStruct((t, d), BF16)],
        scratch_shapes=[pltpu.VMEM((2, TOP_K, tb, d // 2), jnp.uint32), pltpu.SemaphoreType.DMA((2,))],
        compiler_params=_params("arbitrary"),
        name="moe_combine_ln",
    )(pos_flat, pos_flat, y_sorted, gates, xres, g.reshape(1, d), b.reshape(1, d))


def _moe_ln(x, xp, layer, router_w, router_b, w_gu, b_gu, w_d, b_d, g, b, alpha):
    t, d = x.shape
    n_e = router_w.shape[1]
    m = t * TOP_K
    tm = _pick(m, (EXPERT_ROWS, 512, 256, 128, 64, 32, 16))
    tg = _pick(tm, (256, 128, 64, 32, 16))
    idx_kt, gate_kt = _router(x, router_w, router_b)
    flat_e = idx_kt.T.reshape(m)
    order = jnp.argsort(flat_e, stable=True).astype(jnp.int32)
    e_sorted = flat_e[order]
    counts = jnp.sum(flat_e[:, None] == jnp.arange(n_e, dtype=jnp.int32)[None, :], axis=0, dtype=jnp.int32)
    padded = (counts + tm - 1) // tm * tm
    pad_end = jnp.cumsum(padded)
    pad_start = pad_end - padded
    start = jnp.cumsum(counts) - counts
    dest_sorted = pad_start[e_sorted] + jnp.arange(m, dtype=jnp.int32) - start[e_sorted]
    _, pos = lax.sort_key_val(order, dest_sorted)
    n_blocks = m // tm + n_e
    block_expert = jnp.minimum(
        jnp.searchsorted(pad_end, jnp.arange(n_blocks, dtype=jnp.int32) * tm, side='right'), n_e - 1)
    in_run = (jnp.arange(n_blocks, dtype=jnp.int32) * tm - pad_start[block_expert])[:, None] \
        + jnp.arange(tm, dtype=jnp.int32)[None, :]
    src = jnp.minimum(start[block_expert][:, None] + in_run, m - 1)
    slot_tok = jnp.where(in_run < counts[block_expert][:, None], (order // TOP_K)[src], 0).reshape(n_blocks * tm)
    n_used_rows = pad_end[-1]
    xs = _gather_rows(xp, slot_tok, n_used_rows, tg)
    ys = _expert_ffn(xs, block_expert, n_used_rows // tm, layer, w_gu, b_gu, w_d, b_d, tm)
    return _combine_ln(ys, pos.reshape(t, TOP_K), gate_kt.T, x, g, b, alpha)


def _halving_levels(c):
    n_lev = int(np.log2(c))
    assert 2 ** n_lev == c
    idx = np.arange(c)
    upper = np.zeros((n_lev, c, c), np.float32)
    lower = np.zeros((n_lev, c, c), np.float32)
    mask = np.zeros((n_lev, c, c), np.float32)
    for lev in range(n_lev):
        h = 2 ** lev
        blk = idx // (2 * h)
        is_up = (idx % (2 * h)) >= h
        mid = blk * 2 * h + h
        r = idx[None, :]
        upper[lev] = (is_up[:, None] & (r >= mid[:, None]) & (r <= idx[:, None]))
        lower[lev] = ((~is_up)[:, None] & (r > idx[:, None]) & (r <= mid[:, None] - 1))
        mask[lev] = (is_up[:, None] & (~is_up)[None, :] & (blk[:, None] == blk[None, :]))
    return upper, lower, mask


def _split3(x):
    hi = x.astype(BF16)
    r1 = x - hi.astype(F32)
    mid = r1.astype(BF16)
    lo = (r1 - mid.astype(F32)).astype(BF16)
    return hi, mid, lo


def _sel_dot(sel, x):
    hi, mid, lo = _split3(x)
    return (jnp.dot(sel, hi, preferred_element_type=F32) + jnp.dot(sel, mid, preferred_element_type=F32)
            + jnp.dot(sel, lo, preferred_element_type=F32))


def _dot_nt(a, b):
    return lax.dot_general(a, b, (((1,), (1,)), ((), ())), preferred_element_type=F32)


def _dot_tn(a, b):
    return lax.dot_general(a, b, (((0,), (0,)), ((), ())), preferred_element_type=F32)


def _gla_body(q_ref, f_ref, v_ref, g_ref, lb_ref, nw_ref, s0_ref, sums_ref, mask_ref, o_ref, s_out_ref, st_scr,
              *, c, hb, dk, scale):
    n_lev = mask_ref.shape[0]
    chunk = pl.program_id(2)

    @pl.when(chunk == 0)
    def _():
        for h in range(hb):
            st_scr[h] = s0_ref[0, h].T

    sums = sums_ref[...]
    heads = range(hb)
    sls = [slice(h * dk, (h + 1) * dk) for h in heads]
    lbs = [lb_ref[:, sl] for sl in sls]
    sigs = [jax.nn.sigmoid(f_ref[:, sl]) for sl in sls]
    log_f = [jnp.log(lb + (1.0 - lb) * sig) for lb, sig in zip(lbs, sigs)]
    ks = [(1.0 - lb) * (1.0 - sig) for lb, sig in zip(lbs, sigs)]
    qs = [jax.nn.silu(q_ref[:, sl]) * scale for sl in sls]
    vs = [v_ref[:, sl] for sl in sls]
    vbs = [v.astype(BF16) for v in vs]
    cs = [_sel_dot(sums, lf) for lf in log_f]
    sts = [st_scr[h] for h in heads]
    os_ = [_dot_nt((q * jnp.exp(c_[0:c])).astype(BF16), st.astype(BF16)) for q, c_, st in zip(qs, cs, sts)]
    attn = [jnp.zeros((c, c), F32) for _ in heads]
    for lev in range(n_lev):
        up = slice((2 + lev) * c, (3 + lev) * c)
        lo = slice((2 + n_lev + lev) * c, (3 + n_lev + lev) * c)
        qu = [(q * jnp.exp(c_[up])).astype(BF16) for q, c_ in zip(qs, cs)]
        kl = [(k * jnp.exp(c_[lo])).astype(BF16) for k, c_ in zip(ks, cs)]
        attn = [a + mask_ref[lev] * _dot_nt(qu_h, kl_h) for a, qu_h, kl_h in zip(attn, qu, kl)]
    os_ = [o + jnp.dot(a.astype(BF16), vb, preferred_element_type=F32) + jnp.sum(q * k, axis=-1, keepdims=True) * v
           for o, a, vb, q, k, v in zip(os_, attn, vbs, qs, ks, vs)]
    kds = [(k * jnp.exp(c_[c:2 * c])).astype(BF16) for k, c_ in zip(ks, cs)]
    new_st = [st * jnp.exp(c_[c - 1:c, :]) + _dot_tn(vb, kd) for st, c_, vb, kd in zip(sts, cs, vbs, kds)]
    for h in heads:
        st_scr[h] = new_st[h]
        o = os_[h]
        o = o * lax.rsqrt(jnp.mean(o * o, axis=-1, keepdims=True) + RMS_EPS)
        o_ref[:, sls[h]] = (o * nw_ref[...] * jax.nn.silu(g_ref[:, sls[h]])).astype(BF16)

    @pl.when(chunk == pl.num_programs(2) - 1)
    def _():
        for h in range(hb):
            s_out_ref[0, h] = st_scr[h].T


def _gla_mixer(proj, row0, bsz, seq, lb, s0, norm_w):
    _, n_h, dk, dv = s0.shape
    assert dk == dv == V7X_LANES
    c = min(CHUNK, seq)
    n_chunks = seq // c
    hb = _pick(n_h, (8, 4, 2, 1))
    n_hg = n_h // hb
    upper, lower, mask = _halving_levels(c)
    tri = np.tril(np.ones((c, c), np.float32))
    suffix = np.triu(np.ones((c, c), np.float32), 1)
    sums = jnp.asarray(np.concatenate([tri, suffix] + list(upper) + list(lower), axis=0), BF16)
    w = hb * dk
    blk0 = row0 // c
    sec = lambda s: pl.BlockSpec((c, w), lambda b, g, ch: (blk0 + b * n_chunks + ch, s * n_hg + g))
    fixed2 = lambda b, g, ch: (0, 0)
    state_spec = pl.BlockSpec((1, hb, dk, dv), lambda b, g, ch: (b, g, 0, 0))
    out, s_new = pl.pallas_call(
        functools.partial(_gla_body, c=c, hb=hb, dk=dk, scale=float(dk) ** -0.5),
        grid=(bsz, n_hg, n_chunks),
        in_specs=[sec(0), sec(1), sec(2), sec(3),
                  pl.BlockSpec((1, w), lambda b, g, ch: (0, g)),
                  pl.BlockSpec((1, dv), fixed2),
                  state_spec,
                  pl.BlockSpec(sums.shape, fixed2),
                  pl.BlockSpec(mask.shape, lambda b, g, ch: (0, 0, 0))],
        out_specs=[pl.BlockSpec((c, w), lambda b, g, ch: (b * n_chunks + ch, g)), state_spec],
        out_shape=[jax.ShapeDtypeStruct((bsz * seq, n_h * dv), BF16),
                   jax.ShapeDtypeStruct(s0.shape, F32)],
        scratch_shapes=[pltpu.VMEM((hb, dv, dk), F32)],
        compiler_params=_params("parallel", "parallel", "arbitrary"),
        name="gla_mixer",
    )(proj, proj, proj, proj, lb.reshape(1, n_h * dk), norm_w.reshape(1, dv), s0.astype(F32), sums,
      jnp.asarray(mask))
    return out, s_new


def _dot_x3(a, b):
    a_hi = a.astype(BF16)
    b_hi = b.astype(BF16)
    a_lo = (a - a_hi.astype(F32)).astype(BF16)
    b_lo = (b - b_hi.astype(F32)).astype(BF16)
    return (jnp.dot(a_hi, b_hi, preferred_element_type=F32) + jnp.dot(a_hi, b_lo, preferred_element_type=F32)
            + jnp.dot(a_lo, b_hi, preferred_element_type=F32))


def _conv_silu(u_ref, prev_ref, w_ref, bias, cbuf, first, c):
    @pl.when(first)
    def _():
        cbuf[8 - (CONV_W - 1):8, :] = prev_ref[0]

    cbuf[8:8 + c, :] = u_ref[...]
    acc = w_ref[CONV_W - 1:CONV_W, :] * cbuf[8:8 + c, :]
    for j in range(CONV_W - 1):
        off = 8 - (CONV_W - 1) + j
        acc = acc + w_ref[j:j + 1, :] * cbuf[off:off + c, :]
    cbuf[0:8, :] = cbuf[c:c + 8, :]
    if bias is not None:
        acc = acc + bias
    return jax.nn.silu(acc)


def _gdn_body(q_ref, k_ref, v_ref, z_ref, sm_ref, pq_ref, pk_ref, pv_ref, wq_ref, wk_ref, wv_ref,
              alog_ref, dtb_ref, nw_ref, s0_ref, tri_ref, su_ref, mask_ref,
              o_ref, s_out_ref, s_scr, cq, ck, cv, *, c, hb, dk, dv):
    n_lev = mask_ref.shape[0]
    chunk = pl.program_id(2)
    first = chunk == 0

    @pl.when(first)
    def _():
        s_scr[...] = s0_ref[0]

    qs = _conv_silu(q_ref, pq_ref, wq_ref, None, cq, first, c)
    ks = _conv_silu(k_ref, pk_ref, wk_ref, None, ck, first, c)
    vs = _conv_silu(v_ref, pv_ref, wv_ref, None, cv, first, c)
    small = sm_ref[...]
    log_a_all = -jnp.exp(alog_ref[0]) * jax.nn.softplus(small + dtb_ref[0])
    tri = tri_ref[...]
    g_all = _sel_dot(tri, log_a_all)
    row = lax.broadcasted_iota(jnp.int32, (c, c), 0)
    col = lax.broadcasted_iota(jnp.int32, (c, c), 1)
    incl = row >= col
    eye = (row == col).astype(F32)
    heads = range(hb)
    dot = lambda a, b: jnp.dot(a, b, preferred_element_type=F32)
    sl_v = [slice(h * dv, (h + 1) * dv) for h in heads]
    q_ = [qs[:, h * dk:(h + 1) * dk] for h in heads]
    k_ = [ks[:, h * dk:(h + 1) * dk] for h in heads]
    v_ = [vs[:, sl] for sl in sl_v]
    q_ = [q * lax.rsqrt(jnp.sum(q * q, axis=-1, keepdims=True) + 1e-6) * (float(dk) ** -0.5) for q in q_]
    k_ = [k * lax.rsqrt(jnp.sum(k * k, axis=-1, keepdims=True) + 1e-6) for k in k_]
    beta = [jax.nn.sigmoid(small[:, h:h + 1]) for h in heads]
    g_ = [g_all[:, hb + h:hb + h + 1] for h in heads]
    dec = [_sel_dot(tri, log_a_all[:, hb + h:hb + h + 1] * su_ref[...]) for h in heads]
    gamma = [jnp.where(incl, jnp.exp(d), 0.0) for d in dec]
    kb = [k.astype(BF16) for k in k_]
    qb = [q.astype(BF16) for q in q_]
    a_mat = [b * _dot_nt(k, k) * gm for b, k, gm in zip(beta, kb, gamma)]
    qk = [_dot_nt(q, k) * gm for q, k, gm in zip(qb, kb, gamma)]
    t_inv = [eye - mask_ref[0] * a for a in a_mat]
    for lev in range(1, n_lev):
        left = [_dot_x3(t, mask_ref[lev] * a) for t, a in zip(t_inv, a_mat)]
        t_inv = [t - _dot_x3(lf, t) for t, lf in zip(t_inv, left)]
    eg = [jnp.exp(g) for g in g_]
    tb = [t.astype(BF16) for t in t_inv]
    u = [dot(t, (v * b).astype(BF16)) for t, v, b in zip(tb, v_, beta)]
    w = [dot(t, (k * (b * e)).astype(BF16)) for t, k, b, e in zip(tb, k_, beta, eg)]
    s_ = [s_scr[h] for h in heads]
    sb = [s.astype(BF16) for s in s_]
    db = [(u_h - dot(w_h.astype(BF16), s)).astype(BF16) for u_h, w_h, s in zip(u, w, sb)]
    o_ = [dot((q * e).astype(BF16), s) + dot(a.astype(BF16), d) for q, e, s, a, d in zip(q_, eg, sb, qk, db)]
    g_end = [g[c - 1:c, :] for g in g_]
    new_s = [jnp.exp(ge) * s + _dot_tn((k * jnp.exp(ge - g)).astype(BF16), d)
             for ge, s, k, g, d in zip(g_end, s_, k_, g_, db)]
    for h in heads:
        s_scr[h] = new_s[h]
        o = o_[h]
        o = o * lax.rsqrt(jnp.mean(o * o, axis=-1, keepdims=True) + RMS_EPS)
        o_ref[:, sl_v[h]] = (o * nw_ref[...] * jax.nn.silu(z_ref[:, sl_v[h]])).astype(BF16)

    @pl.when(chunk == pl.num_programs(2) - 1)
    def _():
        s_out_ref[0] = s_scr[...]


def _ssd_body(x_ref, b_ref, c_ref, z_ref, sm_ref, px_ref, pb_ref, pc_ref, wx_ref, wb_ref, wc_ref,
              bx_ref, bb_ref, bc_ref, alog_ref, dtb_ref, d_ref, nw_ref, exp_ref, s0_ref, tri_ref, su_ref,
              o_ref, s_out_ref, s_scr, cx, cb_buf, cc_buf, *, c, n_r, p):
    chunk = pl.program_id(2)
    first = chunk == 0

    @pl.when(first)
    def _():
        for r in range(n_r):
            s_scr[:, r * p:(r + 1) * p] = s0_ref[0, r]

    xs = _conv_silu(x_ref, px_ref, wx_ref, bx_ref[...], cx, first, c)
    bm = _conv_silu(b_ref, pb_ref, wb_ref, bb_ref[...], cb_buf, first, c)
    cm = _conv_silu(c_ref, pc_ref, wc_ref, bc_ref[...], cc_buf, first, c)
    dt_all = jax.nn.softplus(sm_ref[...] + dtb_ref[0])
    a_all = -jnp.exp(alog_ref[0]) * dt_all
    tri = tri_ref[...]
    cum_all = _sel_dot(tri, a_all)
    expand = exp_ref[...]
    widen = lambda t: sum(jnp.dot(part, expand, preferred_element_type=F32) for part in _split3(t))
    dt_x = widen(dt_all)
    cum_x = widen(cum_all)
    cend_x = cum_x[c - 1:c, :]
    xdt = xs * dt_x
    row = lax.broadcasted_iota(jnp.int32, (c, c), 0)
    col = lax.broadcasted_iota(jnp.int32, (c, c), 1)
    incl = row >= col
    cmb = cm.astype(BF16)
    bmb = bm.astype(BF16)
    cb = _dot_nt(cmb, bmb)
    s = s_scr[...]
    y = jnp.dot(cmb, s.astype(BF16), preferred_element_type=F32) * jnp.exp(cum_x)
    xdtb = xdt.astype(BF16)
    decs = [_sel_dot(tri, a_all[:, r:r + 1] * su_ref[...]) for r in range(n_r)]
    segs = [jnp.where(incl, jnp.exp(dec), 0.0) for dec in decs]
    parts = [jnp.dot((cb * seg).astype(BF16), xdtb[:, r * p:(r + 1) * p], preferred_element_type=F32)
             for r, seg in enumerate(segs)]
    y = y + jnp.concatenate(parts, axis=-1)
    s_scr[...] = s * jnp.exp(cend_x) + _dot_tn(bmb, (xdt * jnp.exp(cend_x - cum_x)).astype(BF16))
    y = (y + d_ref[...] * xs) * jax.nn.silu(z_ref[...])
    y = y * lax.rsqrt(jnp.mean(y * y, axis=-1, keepdims=True) + RMS_EPS)
    o_ref[...] = (y * nw_ref[...]).astype(BF16)

    @pl.when(chunk == pl.num_programs(2) - 1)
    def _():
        for r in range(n_r):
            s_out_ref[0, r] = s_scr[:, r * p:(r + 1) * p]


def _chunk_consts(c):
    tri = jnp.asarray(np.tril(np.ones((c, c), np.float32)), BF16)
    after = jnp.asarray(np.tril(np.ones((c, c), np.float32), -1))
    return tri, after


def _ab_mixers(proj, lay, row0, bsz, seq, conv_g, s_g, conv_s, s_s, conv_gdn_w, gdn_rows, gdn_norm_w,
               conv_ssd_w, conv_ssd_b, ssd_rows, ssd_d_row, ssd_norm_w, expand):
    c = min(CHUNK, seq)
    n_chunks = seq // c
    blk0 = row0 // c
    tri, su = _chunk_consts(c)
    _, _, mask = _halving_levels(c)
    mask = jnp.asarray(mask)
    rows = lambda b, g, ch: blk0 + b * n_chunks + ch
    fixed2 = lambda b, g, ch: (0, 0)
    fixed3 = lambda b, g, ch: (0, 0, 0)
    nk = CONV_W - 1

    h_a, dk, dv, hb = lay['h_a'], lay['dk_a'], lay['dv_a'], lay['hb_a']
    n_hg = h_a // hb
    wk_, wv_ = hb * dk, hb * dv
    qoff, koff, voff = 0, (h_a * dk) // wk_, (2 * h_a * dk) // wv_
    col = lambda width, base: pl.BlockSpec((c, width), lambda b, g, ch: (rows(b, g, ch), base + g))
    cache = lambda width, base: pl.BlockSpec((1, nk, width), lambda b, g, ch: (b, 0, base + g))
    cw = lambda width, base: pl.BlockSpec((CONV_W, width), lambda b, g, ch: (0, base + g))
    st_a = pl.BlockSpec((1, hb, dk, dv), lambda b, g, ch: (b, g, 0, 0))
    alog_a, dtb_a = gdn_rows
    o_a, s_g_new = pl.pallas_call(
        functools.partial(_gdn_body, c=c, hb=hb, dk=dk, dv=dv),
        grid=(bsz, n_hg, n_chunks),
        in_specs=[col(wk_, qoff), col(wk_, koff), col(wv_, voff),
                  col(wv_, lay['za_off'] // wv_),
                  col(V7X_LANES, lay['small_off'] // V7X_LANES),
                  cache(wk_, qoff), cache(wk_, koff), cache(wv_, voff),
                  cw(wk_, qoff), cw(wk_, koff), cw(wv_, voff),
                  pl.BlockSpec((1, 1, V7X_LANES), lambda b, g, ch: (g, 0, 0)),
                  pl.BlockSpec((1, 1, V7X_LANES), lambda b, g, ch: (g, 0, 0)),
                  pl.BlockSpec((1, dv), fixed2),
                  st_a,
                  pl.BlockSpec((c, c), fixed2), pl.BlockSpec((c, c), fixed2),
                  pl.BlockSpec(mask.shape, fixed3)],
        out_specs=[pl.BlockSpec((c, wv_), lambda b, g, ch: (b * n_chunks + ch, g)), st_a],
        out_shape=[jax.ShapeDtypeStruct((bsz * seq, h_a * dv), BF16), jax.ShapeDtypeStruct(s_g.shape, F32)],
        scratch_shapes=[pltpu.VMEM((hb, dk, dv), F32), pltpu.VMEM((8 + c, wk_), F32),
                        pltpu.VMEM((8 + c, wk_), F32), pltpu.VMEM((8 + c, wv_), F32)],
        compiler_params=_params("parallel", "parallel", "arbitrary"),
        name="gdn_mixer",
    )(proj, proj, proj, proj, proj, conv_g, conv_g, conv_g, conv_gdn_w, conv_gdn_w, conv_gdn_w,
      alog_a, dtb_a, gdn_norm_w.reshape(1, dv), s_g.astype(F32), tri, su, mask)

    h_b, p, n_b, d_inner = lay['h_b'], lay['p_b'], lay['n_b'], lay['d_inner']
    n_r = h_b // G_B
    wx = n_r * p
    xoff = lay['xbc_off'] // wx
    boff = (lay['xbc_off'] + d_inner) // n_b
    coff = (lay['xbc_off'] + d_inner + G_B * n_b) // n_b
    cxo, cbo, cco = 0, d_inner // n_b, (d_inner + G_B * n_b) // n_b
    st_b = pl.BlockSpec((1, n_r, n_b, p), lambda b, g, ch: (b, g, 0, 0))
    alog_b, dtb_b = ssd_rows
    bias2 = conv_ssd_b.reshape(1, -1)
    y_b, s_s_new = pl.pallas_call(
        functools.partial(_ssd_body, c=c, n_r=n_r, p=p),
        grid=(bsz, G_B, n_chunks),
        in_specs=[col(wx, xoff), col(n_b, boff), col(n_b, coff),
                  col(wx, lay['zb_off'] // wx),
                  col(V7X_LANES, lay['small_off'] // V7X_LANES + n_hg),
                  cache(wx, cxo), cache(n_b, cbo), cache(n_b, cco),
                  cw(wx, cxo), cw(n_b, cbo), cw(n_b, cco),
                  pl.BlockSpec((1, wx), lambda b, g, ch: (0, cxo + g)),
                  pl.BlockSpec((1, n_b), lambda b, g, ch: (0, cbo + g)),
                  pl.BlockSpec((1, n_b), lambda b, g, ch: (0, cco + g)),
                  pl.BlockSpec((1, 1, V7X_LANES), lambda b, g, ch: (g, 0, 0)),
                  pl.BlockSpec((1, 1, V7X_LANES), lambda b, g, ch: (g, 0, 0)),
                  pl.BlockSpec((1, wx), lambda b, g, ch: (0, g)),
                  pl.BlockSpec((1, wx), lambda b, g, ch: (0, g)),
                  pl.BlockSpec((V7X_LANES, wx), fixed2),
                  st_b,
                  pl.BlockSpec((c, c), fixed2), pl.BlockSpec((c, c), fixed2)],
        out_specs=[pl.BlockSpec((c, wx), lambda b, g, ch: (b * n_chunks + ch, g)), st_b],
        out_shape=[jax.ShapeDtypeStruct((bsz * seq, d_inner), BF16), jax.ShapeDtypeStruct(s_s.shape, F32)],
        scratch_shapes=[pltpu.VMEM((n_b, wx), F32), pltpu.VMEM((8 + c, wx), F32),
                        pltpu.VMEM((8 + c, n_b), F32), pltpu.VMEM((8 + c, n_b), F32)],
        compiler_params=_params("parallel", "parallel", "arbitrary"),
        name="ssd_mixer",
    )(proj, proj, proj, proj, proj, conv_s, conv_s, conv_s, conv_ssd_w, conv_ssd_w, conv_ssd_w,
      bias2, bias2, bias2, alog_b, dtb_b, ssd_d_row, ssd_norm_w.reshape(1, d_inner), expand,
      s_s.astype(F32), tri, su)
    return o_a, s_g_new, y_b, s_s_new


def _split_cols(t, sizes):
    idx = [int(i) for i in np.cumsum(sizes)[:-1]]
    return jnp.split(t, idx, axis=-1)


def kernel(x_prompt, x_sample, cache_conv_gdn, state_gdn, cache_conv_ssd, state_ssd, state_hgrn, w_in_ab, conv_gdn_w, gdn_a_log, gdn_dt_bias, gdn_norm_w, conv_ssd_w, conv_ssd_b, ssd_a_log, ssd_dt_bias, ssd_d, ssd_norm_w, w_out_ab, w_in_c, hgrn_lower_bounds, hgrn_norm_w, w_out_c, router_w, router_b, expert_w_gate_up, expert_b_gate_up, expert_w_down, expert_b_down, ln_mix_g, ln_mix_b, ln_ffn_g, ln_ffn_b):
    bp, lp, d = x_prompt.shape
    bs, ls, _ = x_sample.shape
    depth = ln_mix_g.shape[0]
    n_pairs = depth // 2
    alpha = float((2 * depth) ** 0.25)
    tp, ts = bp * lp, bs * ls
    h_a, dv_a = gdn_a_log.shape[1], gdn_norm_w.shape[1]
    qkv_a = conv_gdn_w.shape[2]
    dk_a = (qkv_a // h_a - dv_a) // 2
    h_b, d_inner = ssd_a_log.shape[1], ssd_norm_w.shape[1]
    p_b = d_inner // h_b
    xbc_b = conv_ssd_w.shape[2]
    n_b = (xbc_b - d_inner) // (2 * G_B)
    hd_c = hgrn_norm_w.shape[1]
    h_c = d // hd_c
    exp_c = hgrn_lower_bounds.shape[1] // h_c
    assert min(lp, ls) >= CONV_W - 1

    hb_a = _pick(h_a, (8, 4, 2, 1))
    n_hg_a = h_a // hb_a
    r_b = h_b // G_B
    lay = dict(h_a=h_a, dk_a=dk_a, dv_a=dv_a, hb_a=hb_a, h_b=h_b, p_b=p_b, n_b=n_b, d_inner=d_inner,
               za_off=qkv_a, zb_off=qkv_a + h_a * dv_a, xbc_off=qkv_a + h_a * dv_a + d_inner,
               small_off=qkv_a + h_a * dv_a + d_inner + xbc_b)
    assert lay['za_off'] % (hb_a * dv_a) == 0 and lay['zb_off'] % (r_b * p_b) == 0
    assert lay['xbc_off'] % (r_b * p_b) == 0 and (lay['xbc_off'] + d_inner) % n_b == 0
    assert lay['small_off'] % V7X_LANES == 0 and d_inner % n_b == 0
    w_qkv, w_b, w_a, w_za, w_zb, w_xbc, w_dt = _split_cols(
        w_in_ab, [qkv_a, h_a, h_a, h_a * dv_a, d_inner, xbc_b, h_b])
    lane_pad = lambda n: jnp.zeros(w_in_ab.shape[:2] + (n,), w_in_ab.dtype)
    small_cols = []
    for gi in range(n_hg_a):
        hs = slice(gi * hb_a, (gi + 1) * hb_a)
        small_cols += [w_b[..., hs], w_a[..., hs], lane_pad(V7X_LANES - 2 * hb_a)]
    for g in range(G_B):
        small_cols += [w_dt[..., g * r_b:(g + 1) * r_b], lane_pad(V7X_LANES - r_b)]
    n_small = (n_hg_a + G_B) * V7X_LANES
    small_cols.append(lane_pad(-n_small % (2 * V7X_MXU_DIM)))
    w_in_ab_r = jnp.concatenate([w_qkv, w_za, w_zb, w_xbc] + small_cols, axis=-1).astype(BF16)
    on_lanes = lambda t, lo: jnp.pad(t.astype(F32), ((0, 0), (0, 0), (lo, V7X_LANES - lo - t.shape[-1])))[:, :, None, :]
    gdn_rows = (on_lanes(gdn_a_log.reshape(n_pairs, n_hg_a, hb_a), hb_a),
                on_lanes(gdn_dt_bias.reshape(n_pairs, n_hg_a, hb_a), hb_a))
    ssd_rows = (on_lanes(ssd_a_log.reshape(n_pairs, G_B, r_b), 0),
                on_lanes(ssd_dt_bias.reshape(n_pairs, G_B, r_b), 0))
    ssd_d_rows = jnp.repeat(ssd_d.astype(F32), p_b, axis=-1)[:, None, :]
    expand_np = np.zeros((V7X_LANES, r_b * p_b), np.float32)
    for r in range(r_b):
        expand_np[r, r * p_b:(r + 1) * p_b] = 1.0
    expand = jnp.asarray(expand_np, BF16)
    w_in_c_b = w_in_c.astype(BF16)
    w_out_ab_b = w_out_ab.astype(BF16)
    w_out_c_b = w_out_c.astype(BF16)
    w_gu_b = expert_w_gate_up.astype(BF16)
    w_d_b = expert_w_down.astype(BF16)

    lb_w = jax.nn.softmax(hgrn_lower_bounds.astype(F32), axis=0)
    lower_bounds = jnp.cumsum(lb_w, axis=0) - lb_w[0]

    x = jnp.concatenate([x_prompt.reshape(tp, d), x_sample.reshape(ts, d)], axis=0)
    xb = x.astype(BF16)
    zeros = lambda *shape: jnp.zeros(shape, F32)
    new = {key: [] for key in ('cg_p', 'sg_p', 'cs_p', 'ss_p', 'sh_p', 'cg_s', 'sg_s', 'cs_s', 'ss_s', 'sh_s')}
    for layer in range(depth):
        j = layer // 2
        if layer % 2 == 0:
            proj = _matmul(xb, w_in_ab_r[j])
            wts = (conv_gdn_w[j], (gdn_rows[0][j], gdn_rows[1][j]), gdn_norm_w[j], conv_ssd_w[j], conv_ssd_b[j],
                   (ssd_rows[0][j], ssd_rows[1][j]), ssd_d_rows[j], ssd_norm_w[j], expand)
            oa_p, sg, yb_p, ss = _ab_mixers(
                proj, lay, 0, bp, lp, zeros(bp, CONV_W - 1, qkv_a), zeros(bp, h_a, dk_a, dv_a),
                zeros(bp, CONV_W - 1, xbc_b), zeros(bp, h_b, n_b, p_b), *wts)
            new['sg_p'].append(sg)
            new['ss_p'].append(ss)
            oa_s, sg, yb_s, ss = _ab_mixers(
                proj, lay, tp, bs, ls, cache_conv_gdn[j], state_gdn[j], cache_conv_ssd[j], state_ssd[j], *wts)
            new['sg_s'].append(sg)
            new['ss_s'].append(ss)
            nk = CONV_W - 1
            tail_p = jnp.stack([proj[(b + 1) * lp - nk:(b + 1) * lp] for b in range(bp)])
            tail_s = jnp.stack([proj[tp + (b + 1) * ls - nk:tp + (b + 1) * ls] for b in range(bs)])
            xbc_cols = slice(lay['xbc_off'], lay['xbc_off'] + xbc_b)
            new['cg_p'].append(tail_p[..., :qkv_a])
            new['cs_p'].append(tail_p[..., xbc_cols])
            new['cg_s'].append(tail_s[..., :qkv_a])
            new['cs_s'].append(tail_s[..., xbc_cols])
            mixes = [jnp.concatenate([oa_p, oa_s], axis=0), jnp.concatenate([yb_p, yb_s], axis=0)]
            ws = [w_out_ab_b[j, :h_a * dv_a], w_out_ab_b[j, h_a * dv_a:]]
        else:
            proj = _matmul(xb, w_in_c_b[j])
            mix_p, sh = _gla_mixer(proj, 0, bp, lp, lower_bounds[j], zeros(bp, h_c, exp_c, hd_c), hgrn_norm_w[j])
            new['sh_p'].append(sh)
            mix_s, sh = _gla_mixer(proj, tp, bs, ls, lower_bounds[j], state_hgrn[j], hgrn_norm_w[j])
            new['sh_s'].append(sh)
            mixes = [jnp.concatenate([mix_p, mix_s], axis=0)]
            ws = [w_out_c_b[j]]
        x, xp = _outproj_ln(mixes, ws, x, ln_mix_g[layer], ln_mix_b[layer], alpha)
        x, xb = _moe_ln(x, xp, layer, router_w[layer], router_b[layer], w_gu_b, expert_b_gate_up,
                        w_d_b, expert_b_down, ln_ffn_g[layer], ln_ffn_b[layer], alpha)
    st = {key: jnp.stack(val) for key, val in new.items()}
    return (x[:tp].reshape(bp, lp, d), x[tp:].reshape(bs, ls, d),
            st['cg_p'], st['sg_p'], st['cs_p'], st['ss_p'], st['sh_p'],
            st['cg_s'], st['sg_s'], st['cs_s'], st['ss_s'], st['sh_s'])
```

```python
import functools

import jax
import jax.numpy as jnp
import numpy as np
from jax import lax
from jax.experimental import pallas as pl
from jax.experimental.pallas import tpu as pltpu

F32 = jnp.float32
BF16 = jnp.bfloat16

CHUNK = 64
CONV_W = 4
G_B = 2
TOP_K = 4
SWIGLU_LIMIT = 7.0
SWIGLU_ALPHA = 1.702
LN_EPS = 1e-5
RMS_EPS = 1e-6

V7X_LANES = 128
V7X_MXU_DIM = 256
V7X_VMEM_LIMIT_BYTES = 56 * 1024 * 1024

EXPERT_ROWS = 512
EXPERT_FF_COLS = 1024
ROW_DMA_UNROLL = 8


def _pick(n, candidates):
    for c in candidates:
        if n % c == 0:
            return c
    return n


def _params(*sem):
    return pltpu.CompilerParams(dimension_semantics=sem, vmem_limit_bytes=V7X_VMEM_LIMIT_BYTES)


def _matmul_body(x_ref, w_ref, o_ref):
    o_ref[...] = jnp.dot(x_ref[...], w_ref[...], preferred_element_type=F32)


def _matmul(x, w):
    m, k = x.shape
    n = w.shape[1]
    tm = _pick(m, (768, 512, 256, 128, 64, 32, 16, 8))
    tn = _pick(n, (1024, 512, 256, 128))
    return pl.pallas_call(
        _matmul_body,
        grid=(m // tm, n // tn),
        in_specs=[pl.BlockSpec((tm, k), lambda i, j: (i, 0)),
                  pl.BlockSpec((k, tn), lambda i, j: (0, j))],
        out_specs=pl.BlockSpec((tm, tn), lambda i, j: (i, j)),
        out_shape=jax.ShapeDtypeStruct((m, n), F32),
        compiler_params=_params("parallel", "arbitrary"),
        name="dense_proj",
    )(x, w)


def _layer_norm_rows(y, g, b):
    mu = jnp.mean(y, axis=-1, keepdims=True)
    yc = y - mu
    var = jnp.mean(yc * yc, axis=-1, keepdims=True)
    return yc * lax.rsqrt(var + LN_EPS) * g + b


def _pack_halves(y):
    half = y.shape[-1] // 2
    bits = lambda v: lax.bitcast_convert_type(v.astype(BF16).astype(F32), jnp.uint32)
    return (bits(y[:, :half]) >> 16) | bits(y[:, half:])


def _unpack_halves(p):
    lo = lax.bitcast_convert_type(p << 16, F32)
    hi = lax.bitcast_convert_type(p & jnp.uint32(0xFFFF0000), F32)
    return lo, hi


def _outproj_ln_body(*refs, n_in, alpha):
    mix_refs, w_refs = refs[:n_in], refs[n_in:2 * n_in]
    xres_ref, g_ref, b_ref, of_ref, op_ref = refs[2 * n_in:]
    h = jnp.dot(mix_refs[0][...], w_refs[0][...], preferred_element_type=F32)
    for mix_ref, w_ref in zip(mix_refs[1:], w_refs[1:]):
        h = h + jnp.dot(mix_ref[...], w_ref[...], preferred_element_type=F32)
    y = _layer_norm_rows(alpha * xres_ref[...] + h, g_ref[...], b_ref[...])
    of_ref[...] = y
    op_ref[...] = _pack_halves(y)


def _outproj_ln(mixes, ws, xres, g, b, alpha):
    m, d = xres.shape
    tm = _pick(m, (384, 256, 128, 64, 32, 16, 8))
    row = lambda i: (i, 0)
    fixed = lambda i: (0, 0)
    return pl.pallas_call(
        functools.partial(_outproj_ln_body, n_in=len(mixes), alpha=alpha),
        grid=(m // tm,),
        in_specs=([pl.BlockSpec((tm, mix.shape[1]), row) for mix in mixes]
                  + [pl.BlockSpec(w.shape, fixed) for w in ws]
                  + [pl.BlockSpec((tm, d), row), pl.BlockSpec((1, d), fixed), pl.BlockSpec((1, d), fixed)]),
        out_specs=[pl.BlockSpec((tm, d), row), pl.BlockSpec((tm, d // 2), row)],
        out_shape=[jax.ShapeDtypeStruct((m, d), F32), jax.ShapeDtypeStruct((m, d // 2), jnp.uint32)],
        compiler_params=_params("parallel"),
        name="outproj_ln",
    )(*mixes, *ws, xres, g.reshape(1, d), b.reshape(1, d))


def _router_body(x_ref, wt_ref, b_ref, idx_ref, gate_ref):
    logits = lax.dot_general(wt_ref[...], x_ref[...], (((1,), (1,)), ((), ())),
                             precision=lax.Precision.HIGHEST, preferred_element_type=F32)
    logits = logits + b_ref[...]
    n_e = logits.shape[0]
    iota = lax.broadcasted_iota(jnp.int32, logits.shape, 0)
    vals, idxs = [], []
    for _ in range(TOP_K):
        m = jnp.max(logits, axis=0, keepdims=True)
        sel = jnp.min(jnp.where(logits == m, iota, n_e), axis=0, keepdims=True)
        vals.append(m)
        idxs.append(sel)
        logits = jnp.where(iota == sel, -jnp.inf, logits)
    exps = [jnp.exp(v - vals[0]) for v in vals]
    denom = exps[0] + exps[1] + exps[2] + exps[3]
    idx_ref[...] = jnp.concatenate(idxs, axis=0)
    gate_ref[...] = jnp.concatenate(exps, axis=0) / denom


def _router(x, router_w, router_b):
    t, d = x.shape
    n_e = router_w.shape[1]
    tb = _pick(t, (768, 512, 256, 128))
    return pl.pallas_call(
        _router_body,
        grid=(t // tb,),
        in_specs=[pl.BlockSpec((tb, d), lambda i: (i, 0)),
                  pl.BlockSpec((n_e, d), lambda i: (0, 0)),
                  pl.BlockSpec((n_e, 1), lambda i: (0, 0))],
        out_specs=[pl.BlockSpec((TOP_K, tb), lambda i: (0, i)),
                   pl.BlockSpec((TOP_K, tb), lambda i: (0, i))],
        out_shape=[jax.ShapeDtypeStruct((TOP_K, t), jnp.int32),
                   jax.ShapeDtypeStruct((TOP_K, t), F32)],
        compiler_params=_params("parallel"),
        name="moe_router",
    )(x, router_w.T, router_b.reshape(n_e, 1))


def _row_copy(src_hbm, row, dst, dst_row, sem):
    return pltpu.make_async_copy(src_hbm.at[pl.ds(row, 1), :], dst.at[pl.ds(dst_row, 1), :], sem)


def _gather_body(nused_ref, tok_cur, tok_nxt, x_hbm, o_ref, buf, sem, *, tg):
    i = pl.program_id(0)
    n_used = nused_ref[0]
    slot = i % 2

    def issue(tok_ref, s):
        def body(r, c):
            _row_copy(x_hbm, tok_ref[r], buf.at[s], r, sem.at[s]).start()
            return c
        lax.fori_loop(0, tg, body, 0, unroll=ROW_DMA_UNROLL)

    @pl.when(jnp.logical_and(i == 0, n_used > 0))
    def _():
        issue(tok_cur, 0)

    @pl.when(i + 1 < n_used)
    def _():
        issue(tok_nxt, 1 - slot)

    @pl.when(i < n_used)
    def _():
        pltpu.make_async_copy(x_hbm.at[pl.ds(0, tg), :], buf.at[slot], sem.at[slot]).wait()
        lo, hi = _unpack_halves(buf[slot])
        half = lo.shape[-1]
        o_ref[:, :half] = lo.astype(BF16)
        o_ref[:, half:] = hi.astype(BF16)

    @pl.when(i >= n_used)
    def _():
        o_ref[...] = jnp.zeros_like(o_ref)


def _gather_rows(xp, slot_tok, n_used_rows, tg):
    t, half = xp.shape
    d = 2 * half
    m_pad = slot_tok.shape[0]
    nblk = m_pad // tg
    n_used = (n_used_rows // tg).astype(jnp.int32).reshape(1)
    last = lambda i, nu: jnp.maximum(jnp.minimum(i, nu[0] - 1), 0)
    grid_spec = pltpu.PrefetchScalarGridSpec(
        num_scalar_prefetch=1,
        grid=(nblk,),
        in_specs=[pl.BlockSpec((tg,), lambda i, nu: (last(i, nu),), memory_space=pltpu.SMEM),
                  pl.BlockSpec((tg,), lambda i, nu: (last(i + 1, nu),), memory_space=pltpu.SMEM),
                  pl.BlockSpec(memory_space=pl.ANY)],
        out_specs=pl.BlockSpec((tg, d), lambda i, nu: (i, 0)),
        scratch_shapes=[pltpu.VMEM((2, tg, half), jnp.uint32), pltpu.SemaphoreType.DMA((2,))],
    )
    return pl.pallas_call(
        functools.partial(_gather_body, tg=tg),
        grid_spec=grid_spec,
        out_shape=jax.ShapeDtypeStruct((m_pad, d), BF16),
        compiler_params=_params("arbitrary"),
        name="moe_gather",
    )(n_used, slot_tok, slot_tok, xp)


def _expert_body(meta_ref, x_ref, wg_ref, wl_ref, wd_ref, bg_ref, bl_ref, bd_ref, o_ref, acc_ref):
    i = pl.program_id(0)
    j = pl.program_id(1)

    @pl.when(i < meta_ref[0])
    def _():
        x = x_ref[...]
        glu = jnp.dot(x, wg_ref[0, 0], preferred_element_type=F32) + bg_ref[0, 0]
        lin = jnp.dot(x, wl_ref[0, 0], preferred_element_type=F32) + bl_ref[0, 0]
        glu = jnp.minimum(glu, SWIGLU_LIMIT)
        lin = jnp.clip(lin, -SWIGLU_LIMIT, SWIGLU_LIMIT)
        act = glu * jax.nn.sigmoid(SWIGLU_ALPHA * glu) * (lin + 1.0)
        part = jnp.dot(act.astype(BF16), wd_ref[0, 0], preferred_element_type=F32)

        @pl.when(j == 0)
        def _():
            acc_ref[...] = part + bd_ref[0, 0]

        @pl.when(j > 0)
        def _():
            acc_ref[...] += part

        @pl.when(j == pl.num_programs(1) - 1)
        def _():
            o_ref[...] = _pack_halves(acc_ref[...])

    @pl.when(jnp.logical_and(i >= meta_ref[0], j == 0))
    def _():
        o_ref[...] = jnp.zeros_like(o_ref)


def _expert_ffn(xs, block_expert, n_used_blocks, layer, w_gu, b_gu, w_d, b_d, tm):
    m_pad, d = xs.shape
    depth, n_e, _, two_f = w_gu.shape
    d_ff = two_f // 2
    tf = _pick(d_ff, (EXPERT_FF_COLS, 512, 256, 128))
    n_f = d_ff // tf
    nblk = m_pad // tm
    meta = jnp.concatenate([n_used_blocks.reshape(1).astype(jnp.int32), block_expert.astype(jnp.int32)])

    def blk(i, meta):
        return jnp.maximum(jnp.minimum(i, meta[0] - 1), 0)

    def fj(i, j, meta):
        return jnp.where(i < meta[0], j, n_f - 1)

    def expert(i, meta):
        return meta[1 + blk(i, meta)]

    grid_spec = pltpu.PrefetchScalarGridSpec(
        num_scalar_prefetch=1,
        grid=(nblk, n_f),
        in_specs=[
            pl.BlockSpec((tm, d), lambda i, j, m: (blk(i, m), 0)),
            pl.BlockSpec((1, 1, d, tf), lambda i, j, m: (layer, expert(i, m), 0, fj(i, j, m))),
            pl.BlockSpec((1, 1, d, tf), lambda i, j, m: (layer, expert(i, m), 0, n_f + fj(i, j, m))),
            pl.BlockSpec((1, 1, tf, d), lambda i, j, m: (layer, expert(i, m), fj(i, j, m), 0)),
            pl.BlockSpec((1, 1, 1, tf), lambda i, j, m: (layer, expert(i, m), 0, fj(i, j, m))),
            pl.BlockSpec((1, 1, 1, tf), lambda i, j, m: (layer, expert(i, m), 0, n_f + fj(i, j, m))),
            pl.BlockSpec((1, 1, 1, d), lambda i, j, m: (layer, expert(i, m), 0, 0)),
        ],
        out_specs=pl.BlockSpec((tm, d // 2), lambda i, j, m: (i, 0)),
        scratch_shapes=[pltpu.VMEM((tm, d), F32)],
    )
    return pl.pallas_call(
        _expert_body,
        grid_spec=grid_spec,
        out_shape=jax.ShapeDtypeStruct((m_pad, d // 2), jnp.uint32),
        compiler_params=_params("arbitrary", "arbitrary"),
        name="moe_expert_ffn",
    )(meta, xs, w_gu, w_gu, w_d, b_gu.reshape(depth, n_e, 1, two_f), b_gu.reshape(depth, n_e, 1, two_f),
      b_d.reshape(depth, n_e, 1, d))


def _combine_body(pos_cur, pos_nxt, y_hbm, gate_ref, xres_ref, g_ref, b_ref, of_ref, ob_ref, buf, sem,
                  *, tb, alpha):
    i = pl.program_id(0)
    nblk = pl.num_programs(0)
    slot = i % 2

    def issue(pos_ref, s):
        for k in range(TOP_K):
            def body(t, c, k=k):
                _row_copy(y_hbm, pos_ref[t * TOP_K + k], buf.at[s, k], t, sem.at[s]).start()
                return c
            lax.fori_loop(0, tb, body, 0, unroll=ROW_DMA_UNROLL)

    @pl.when(i == 0)
    def _():
        issue(pos_cur, 0)

    @pl.when(i + 1 < nblk)
    def _():
        issue(pos_nxt, 1 - slot)

    for k in range(TOP_K):
        pltpu.make_async_copy(y_hbm.at[pl.ds(0, tb), :], buf.at[slot, k], sem.at[slot]).wait()
    gate = gate_ref[...]
    f_lo = f_hi = None
    for k in range(TOP_K):
        lo, hi = _unpack_halves(buf[slot, k])
        gk = gate[:, k:k + 1]
        f_lo = gk * lo if f_lo is None else f_lo + gk * lo
        f_hi = gk * hi if f_hi is None else f_hi + gk * hi
    f = jnp.concatenate([f_lo, f_hi], axis=-1)
    y = _layer_norm_rows(alpha * xres_ref[...] + f, g_ref[...], b_ref[...])
    of_ref[...] = y
    ob_ref[...] = y.astype(BF16)


def _combine_ln(y_sorted, pos, gates, xres, g, b, alpha):
    t, d = xres.shape
    tb = _pick(t, (256, 128, 64, 32, 16, 8))
    nblk = t // tb
    row = lambda i: (i, 0)
    fixed = lambda i: (0, 0)
    pos_flat = pos.reshape(t * TOP_K)
    return pl.pallas_call(
        functools.partial(_combine_body, tb=tb, alpha=alpha),
        grid=(nblk,),
        in_specs=[pl.BlockSpec((tb * TOP_K,), lambda i: (i,), memory_space=pltpu.SMEM),
                  pl.BlockSpec((tb * TOP_K,), lambda i: (jnp.minimum(i + 1, nblk - 1),), memory_space=pltpu.SMEM),
                  pl.BlockSpec(memory_space=pl.ANY),
                  pl.BlockSpec((tb, TOP_K), row),
                  pl.BlockSpec((tb, d), row),
                  pl.BlockSpec((1, d), fixed),
                  pl.BlockSpec((1, d), fixed)],
        out_specs=[pl.BlockSpec((tb, d), row), pl.BlockSpec((tb, d), row)],
        out_shape=[jax.ShapeDtypeStruct((t, d), F32), jax.ShapeDtypeStruct((t, d), BF16)],
        scratch_shapes=[pltpu.VMEM((2, TOP_K, tb, d // 2), jnp.uint32), pltpu.SemaphoreType.DMA((2,))],
        compiler_params=_params("arbitrary"),
        name="moe_combine_ln",
    )(pos_flat, pos_flat, y_sorted, gates, xres, g.reshape(1, d), b.reshape(1, d))


def _moe_ln(x, xp, layer, router_w, router_b, w_gu, b_gu, w_d, b_d, g, b, alpha):
    t, d = x.shape
    n_e = router_w.shape[1]
    m = t * TOP_K
    tm = _pick(m, (EXPERT_ROWS, 512, 256, 128, 64, 32, 16))
    tg = _pick(tm, (256, 128, 64, 32, 16))
    idx_kt, gate_kt = _router(x, router_w, router_b)
    flat_e = idx_kt.T.reshape(m)
    order = jnp.argsort(flat_e, stable=True).astype(jnp.int32)
    e_sorted = flat_e[order]
    counts = jnp.sum(flat_e[:, None] == jnp.arange(n_e, dtype=jnp.int32)[None, :], axis=0, dtype=jnp.int32)
    padded = (counts + tm - 1) // tm * tm
    pad_end = jnp.cumsum(padded)
    pad_start = pad_end - padded
    start = jnp.cumsum(counts) - counts
    dest_sorted = pad_start[e_sorted] + jnp.arange(m, dtype=jnp.int32) - start[e_sorted]
    _, pos = lax.sort_key_val(order, dest_sorted)
    n_blocks = m // tm + n_e
    block_row0 = jnp.arange(n_blocks, dtype=jnp.int32) * tm
    block_expert = jnp.minimum(
        jnp.sum(pad_end[None, :] <= block_row0[:, None], axis=1, dtype=jnp.int32), n_e - 1)
    in_run = (jnp.arange(n_blocks, dtype=jnp.int32) * tm - pad_start[block_expert])[:, None] \
        + jnp.arange(tm, dtype=jnp.int32)[None, :]
    src = jnp.minimum(start[block_expert][:, None] + in_run, m - 1)
    slot_tok = jnp.where(in_run < counts[block_expert][:, None], (order // TOP_K)[src], 0).reshape(n_blocks * tm)
    n_used_rows = pad_end[-1]
    xs = _gather_rows(xp, slot_tok, n_used_rows, tg)
    ys = _expert_ffn(xs, block_expert, n_used_rows // tm, layer, w_gu, b_gu, w_d, b_d, tm)
    return _combine_ln(ys, pos.reshape(t, TOP_K), gate_kt.T, x, g, b, alpha)


def _halving_levels(c):
    n_lev = int(np.log2(c))
    assert 2 ** n_lev == c
    idx = np.arange(c)
    upper = np.zeros((n_lev, c, c), np.float32)
    lower = np.zeros((n_lev, c, c), np.float32)
    mask = np.zeros((n_lev, c, c), np.float32)
    for lev in range(n_lev):
        h = 2 ** lev
        blk = idx // (2 * h)
        is_up = (idx % (2 * h)) >= h
        mid = blk * 2 * h + h
        r = idx[None, :]
        upper[lev] = (is_up[:, None] & (r >= mid[:, None]) & (r <= idx[:, None]))
        lower[lev] = ((~is_up)[:, None] & (r > idx[:, None]) & (r <= mid[:, None] - 1))
        mask[lev] = (is_up[:, None] & (~is_up)[None, :] & (blk[:, None] == blk[None, :]))
    return upper, lower, mask


def _split3(x):
    hi = x.astype(BF16)
    r1 = x - hi.astype(F32)
    mid = r1.astype(BF16)
    lo = (r1 - mid.astype(F32)).astype(BF16)
    return hi, mid, lo


def _sel_dot(sel, x):
    hi, mid, lo = _split3(x)
    return (jnp.dot(sel, hi, preferred_element_type=F32) + jnp.dot(sel, mid, preferred_element_type=F32)
            + jnp.dot(sel, lo, preferred_element_type=F32))


def _dot_nt(a, b):
    return lax.dot_general(a, b, (((1,), (1,)), ((), ())), preferred_element_type=F32)


def _dot_tn(a, b):
    return lax.dot_general(a, b, (((0,), (0,)), ((), ())), preferred_element_type=F32)


def _gla_body(q_ref, f_ref, v_ref, g_ref, lb_ref, nw_ref, s0_ref, sums_ref, mask_ref, o_ref, s_out_ref, st_scr,
              *, c, hb, dk, scale):
    n_lev = mask_ref.shape[0]
    chunk = pl.program_id(2)

    @pl.when(chunk == 0)
    def _():
        for h in range(hb):
            st_scr[h] = s0_ref[0, h].T

    sums = sums_ref[...]
    heads = range(hb)
    sls = [slice(h * dk, (h + 1) * dk) for h in heads]
    lbs = [lb_ref[:, sl] for sl in sls]
    sigs = [jax.nn.sigmoid(f_ref[:, sl]) for sl in sls]
    log_f = [jnp.log(lb + (1.0 - lb) * sig) for lb, sig in zip(lbs, sigs)]
    ks = [(1.0 - lb) * (1.0 - sig) for lb, sig in zip(lbs, sigs)]
    qs = [jax.nn.silu(q_ref[:, sl]) * scale for sl in sls]
    vs = [v_ref[:, sl] for sl in sls]
    vbs = [v.astype(BF16) for v in vs]
    cs = [_sel_dot(sums, lf) for lf in log_f]
    sts = [st_scr[h] for h in heads]
    os_ = [_dot_nt((q * jnp.exp(c_[0:c])).astype(BF16), st.astype(BF16)) for q, c_, st in zip(qs, cs, sts)]
    attn = [jnp.zeros((c, c), F32) for _ in heads]
    for lev in range(n_lev):
        fac = [jnp.exp(c_[(2 + lev) * c:(3 + lev) * c]) for c_ in cs]
        qu = [(q * e).astype(BF16) for q, e in zip(qs, fac)]
        kl = [(k * e).astype(BF16) for k, e in zip(ks, fac)]
        attn = [a + mask_ref[lev] * _dot_nt(qu_h, kl_h) for a, qu_h, kl_h in zip(attn, qu, kl)]
    os_ = [o + jnp.dot(a.astype(BF16), vb, preferred_element_type=F32) + jnp.sum(q * k, axis=-1, keepdims=True) * v
           for o, a, vb, q, k, v in zip(os_, attn, vbs, qs, ks, vs)]
    kds = [(k * jnp.exp(c_[c:2 * c])).astype(BF16) for k, c_ in zip(ks, cs)]
    new_st = [st * jnp.exp(c_[c - 1:c, :]) + _dot_tn(vb, kd) for st, c_, vb, kd in zip(sts, cs, vbs, kds)]
    for h in heads:
        st_scr[h] = new_st[h]
        o = os_[h]
        o = o * lax.rsqrt(jnp.mean(o * o, axis=-1, keepdims=True) + RMS_EPS)
        o_ref[:, sls[h]] = (o * nw_ref[...] * jax.nn.silu(g_ref[:, sls[h]])).astype(BF16)

    @pl.when(chunk == pl.num_programs(2) - 1)
    def _():
        for h in range(hb):
            s_out_ref[0, h] = st_scr[h].T


def _gla_mixer(proj, row0, bsz, seq, lb, s0, norm_w):
    _, n_h, dk, dv = s0.shape
    assert dk == dv == V7X_LANES
    c = min(CHUNK, seq)
    n_chunks = seq // c
    hb = _pick(n_h, (8, 4, 2, 1))
    n_hg = n_h // hb
    upper, lower, mask = _halving_levels(c)
    tri = np.tril(np.ones((c, c), np.float32))
    suffix = np.triu(np.ones((c, c), np.float32), 1)
    sums = jnp.asarray(np.concatenate([tri, suffix] + [u + lo for u, lo in zip(upper, lower)], axis=0), BF16)
    w = hb * dk
    blk0 = row0 // c
    sec = lambda s: pl.BlockSpec((c, w), lambda b, g, ch: (blk0 + b * n_chunks + ch, s * n_hg + g))
    fixed2 = lambda b, g, ch: (0, 0)
    state_spec = pl.BlockSpec((1, hb, dk, dv), lambda b, g, ch: (b, g, 0, 0))
    out, s_new = pl.pallas_call(
        functools.partial(_gla_body, c=c, hb=hb, dk=dk, scale=float(dk) ** -0.5),
        grid=(bsz, n_hg, n_chunks),
        in_specs=[sec(0), sec(1), sec(2), sec(3),
                  pl.BlockSpec((1, w), lambda b, g, ch: (0, g)),
                  pl.BlockSpec((1, dv), fixed2),
                  state_spec,
                  pl.BlockSpec(sums.shape, fixed2),
                  pl.BlockSpec(mask.shape, lambda b, g, ch: (0, 0, 0))],
        out_specs=[pl.BlockSpec((c, w), lambda b, g, ch: (b * n_chunks + ch, g)), state_spec],
        out_shape=[jax.ShapeDtypeStruct((bsz * seq, n_h * dv), BF16),
                   jax.ShapeDtypeStruct(s0.shape, F32)],
        scratch_shapes=[pltpu.VMEM((hb, dv, dk), F32)],
        compiler_params=_params("parallel", "parallel", "arbitrary"),
        name="gla_mixer",
    )(proj, proj, proj, proj, lb.reshape(1, n_h * dk), norm_w.reshape(1, dv), s0.astype(F32), sums,
      jnp.asarray(mask))
    return out, s_new


def _dot_x3(a, b):
    a_hi = a.astype(BF16)
    b_hi = b.astype(BF16)
    a_lo = (a - a_hi.astype(F32)).astype(BF16)
    b_lo = (b - b_hi.astype(F32)).astype(BF16)
    return (jnp.dot(a_hi, b_hi, preferred_element_type=F32) + jnp.dot(a_hi, b_lo, preferred_element_type=F32)
            + jnp.dot(a_lo, b_hi, preferred_element_type=F32))


def _conv_silu(u_ref, prev_ref, w_ref, bias, cbuf, first, c):
    @pl.when(first)
    def _():
        cbuf[8 - (CONV_W - 1):8, :] = prev_ref[0]

    cbuf[8:8 + c, :] = u_ref[...]
    acc = w_ref[CONV_W - 1:CONV_W, :] * cbuf[8:8 + c, :]
    for j in range(CONV_W - 1):
        off = 8 - (CONV_W - 1) + j
        acc = acc + w_ref[j:j + 1, :] * cbuf[off:off + c, :]
    cbuf[0:8, :] = cbuf[c:c + 8, :]
    if bias is not None:
        acc = acc + bias
    return jax.nn.silu(acc)


def _gdn_body(q_ref, k_ref, v_ref, z_ref, sm_ref, pq_ref, pk_ref, pv_ref, wq_ref, wk_ref, wv_ref,
              alog_ref, dtb_ref, nw_ref, s0_ref, tri_ref, su_ref, mask_ref,
              o_ref, s_out_ref, s_scr, cq, ck, cv, *, c, hb, dk, dv):
    n_lev = mask_ref.shape[0]
    chunk = pl.program_id(2)
    first = chunk == 0

    @pl.when(first)
    def _():
        s_scr[...] = s0_ref[0]

    qs = _conv_silu(q_ref, pq_ref, wq_ref, None, cq, first, c)
    ks = _conv_silu(k_ref, pk_ref, wk_ref, None, ck, first, c)
    vs = _conv_silu(v_ref, pv_ref, wv_ref, None, cv, first, c)
    small = sm_ref[...]
    log_a_all = -jnp.exp(alog_ref[0]) * jax.nn.softplus(small + dtb_ref[0])
    tri = tri_ref[...]
    g_all = _sel_dot(tri, log_a_all)
    row = lax.broadcasted_iota(jnp.int32, (c, c), 0)
    col = lax.broadcasted_iota(jnp.int32, (c, c), 1)
    incl = row >= col
    eye = (row == col).astype(F32)
    heads = range(hb)
    dot = lambda a, b: jnp.dot(a, b, preferred_element_type=F32)
    sl_v = [slice(h * dv, (h + 1) * dv) for h in heads]
    q_ = [qs[:, h * dk:(h + 1) * dk] for h in heads]
    k_ = [ks[:, h * dk:(h + 1) * dk] for h in heads]
    v_ = [vs[:, sl] for sl in sl_v]
    q_ = [q * lax.rsqrt(jnp.sum(q * q, axis=-1, keepdims=True) + 1e-6) * (float(dk) ** -0.5) for q in q_]
    k_ = [k * lax.rsqrt(jnp.sum(k * k, axis=-1, keepdims=True) + 1e-6) for k in k_]
    beta = [jax.nn.sigmoid(small[:, h:h + 1]) for h in heads]
    g_ = [g_all[:, hb + h:hb + h + 1] for h in heads]
    dec = [_sel_dot(tri, log_a_all[:, hb + h:hb + h + 1] * su_ref[...]) for h in heads]
    gamma = [jnp.where(incl, jnp.exp(d), 0.0) for d in dec]
    kb = [k.astype(BF16) for k in k_]
    qb = [q.astype(BF16) for q in q_]
    a_mat = [b * _dot_nt(k, k) * gm for b, k, gm in zip(beta, kb, gamma)]
    qk = [_dot_nt(q, k) * gm for q, k, gm in zip(qb, kb, gamma)]
    t_inv = [eye - mask_ref[0] * a for a in a_mat]
    for lev in range(1, n_lev):
        left = [_dot_x3(t, mask_ref[lev] * a) for t, a in zip(t_inv, a_mat)]
        t_inv = [t - _dot_x3(lf, t) for t, lf in zip(t_inv, left)]
    eg = [jnp.exp(g) for g in g_]
    tb = [t.astype(BF16) for t in t_inv]
    u = [dot(t, (v * b).astype(BF16)) for t, v, b in zip(tb, v_, beta)]
    w = [dot(t, (k * (b * e)).astype(BF16)) for t, k, b, e in zip(tb, k_, beta, eg)]
    s_ = [s_scr[h] for h in heads]
    sb = [s.astype(BF16) for s in s_]
    db = [(u_h - dot(w_h.astype(BF16), s)).astype(BF16) for u_h, w_h, s in zip(u, w, sb)]
    o_ = [dot((q * e).astype(BF16), s) + dot(a.astype(BF16), d) for q, e, s, a, d in zip(q_, eg, sb, qk, db)]
    g_end = [g[c - 1:c, :] for g in g_]
    new_s = [jnp.exp(ge) * s + _dot_tn((k * jnp.exp(ge - g)).astype(BF16), d)
             for ge, s, k, g, d in zip(g_end, s_, k_, g_, db)]
    for h in heads:
        s_scr[h] = new_s[h]
        o = o_[h]
        o = o * lax.rsqrt(jnp.mean(o * o, axis=-1, keepdims=True) + RMS_EPS)
        o_ref[:, sl_v[h]] = (o * nw_ref[...] * jax.nn.silu(z_ref[:, sl_v[h]])).astype(BF16)

    @pl.when(chunk == pl.num_programs(2) - 1)
    def _():
        s_out_ref[0] = s_scr[...]


def _ssd_body(x_ref, b_ref, c_ref, z_ref, sm_ref, px_ref, pb_ref, pc_ref, wx_ref, wb_ref, wc_ref,
              bx_ref, bb_ref, bc_ref, alog_ref, dtb_ref, d_ref, nw_ref, exp_ref, s0_ref, tri_ref, su_ref,
              o_ref, s_out_ref, s_scr, cx, cb_buf, cc_buf, *, c, n_r, p):
    chunk = pl.program_id(2)
    first = chunk == 0

    @pl.when(first)
    def _():
        for r in range(n_r):
            s_scr[:, r * p:(r + 1) * p] = s0_ref[0, r]

    xs = _conv_silu(x_ref, px_ref, wx_ref, bx_ref[...], cx, first, c)
    bm = _conv_silu(b_ref, pb_ref, wb_ref, bb_ref[...], cb_buf, first, c)
    cm = _conv_silu(c_ref, pc_ref, wc_ref, bc_ref[...], cc_buf, first, c)
    dt_all = jax.nn.softplus(sm_ref[...] + dtb_ref[0])
    a_all = -jnp.exp(alog_ref[0]) * dt_all
    tri = tri_ref[...]
    cum_all = _sel_dot(tri, a_all)
    expand = exp_ref[...]
    widen = lambda t: sum(jnp.dot(part, expand, preferred_element_type=F32) for part in _split3(t))
    dt_x = widen(dt_all)
    cum_x = widen(cum_all)
    cend_x = cum_x[c - 1:c, :]
    xdt = xs * dt_x
    row = lax.broadcasted_iota(jnp.int32, (c, c), 0)
    col = lax.broadcasted_iota(jnp.int32, (c, c), 1)
    incl = row >= col
    cmb = cm.astype(BF16)
    bmb = bm.astype(BF16)
    cb = _dot_nt(cmb, bmb)
    s = s_scr[...]
    y = jnp.dot(cmb, s.astype(BF16), preferred_element_type=F32) * jnp.exp(cum_x)
    xdtb = xdt.astype(BF16)
    decs = [_sel_dot(tri, a_all[:, r:r + 1] * su_ref[...]) for r in range(n_r)]
    segs = [jnp.where(incl, jnp.exp(dec), 0.0) for dec in decs]
    parts = [jnp.dot((cb * seg).astype(BF16), xdtb[:, r * p:(r + 1) * p], preferred_element_type=F32)
             for r, seg in enumerate(segs)]
    y = y + jnp.concatenate(parts, axis=-1)
    s_scr[...] = s * jnp.exp(cend_x) + _dot_tn(bmb, (xdt * jnp.exp(cend_x - cum_x)).astype(BF16))
    y = (y + d_ref[...] * xs) * jax.nn.silu(z_ref[...])
    y = y * lax.rsqrt(jnp.mean(y * y, axis=-1, keepdims=True) + RMS_EPS)
    o_ref[...] = (y * nw_ref[...]).astype(BF16)

    @pl.when(chunk == pl.num_programs(2) - 1)
    def _():
        for r in range(n_r):
            s_out_ref[0, r] = s_scr[:, r * p:(r + 1) * p]


def _chunk_consts(c):
    tri = jnp.asarray(np.tril(np.ones((c, c), np.float32)), BF16)
    after = jnp.asarray(np.tril(np.ones((c, c), np.float32), -1))
    return tri, after


def _ab_mixers(proj, lay, row0, bsz, seq, conv_g, s_g, conv_s, s_s, conv_gdn_w, gdn_rows, gdn_norm_w,
               conv_ssd_w, conv_ssd_b, ssd_rows, ssd_d_row, ssd_norm_w, expand):
    c = min(CHUNK, seq)
    n_chunks = seq // c
    blk0 = row0 // c
    tri, su = _chunk_consts(c)
    _, _, mask = _halving_levels(c)
    mask = jnp.asarray(mask)
    rows = lambda b, g, ch: blk0 + b * n_chunks + ch
    fixed2 = lambda b, g, ch: (0, 0)
    fixed3 = lambda b, g, ch: (0, 0, 0)
    nk = CONV_W - 1

    h_a, dk, dv, hb = lay['h_a'], lay['dk_a'], lay['dv_a'], lay['hb_a']
    n_hg = h_a // hb
    wk_, wv_ = hb * dk, hb * dv
    qoff, koff, voff = 0, (h_a * dk) // wk_, (2 * h_a * dk) // wv_
    col = lambda width, base: pl.BlockSpec((c, width), lambda b, g, ch: (rows(b, g, ch), base + g))
    cache = lambda width, base: pl.BlockSpec((1, nk, width), lambda b, g, ch: (b, 0, base + g))
    cw = lambda width, base: pl.BlockSpec((CONV_W, width), lambda b, g, ch: (0, base + g))
    st_a = pl.BlockSpec((1, hb, dk, dv), lambda b, g, ch: (b, g, 0, 0))
    alog_a, dtb_a = gdn_rows
    o_a, s_g_new = pl.pallas_call(
        functools.partial(_gdn_body, c=c, hb=hb, dk=dk, dv=dv),
        grid=(bsz, n_hg, n_chunks),
        in_specs=[col(wk_, qoff), col(wk_, koff), col(wv_, voff),
                  col(wv_, lay['za_off'] // wv_),
                  col(V7X_LANES, lay['small_off'] // V7X_LANES),
                  cache(wk_, qoff), cache(wk_, koff), cache(wv_, voff),
                  cw(wk_, qoff), cw(wk_, koff), cw(wv_, voff),
                  pl.BlockSpec((1, 1, V7X_LANES), lambda b, g, ch: (g, 0, 0)),
                  pl.BlockSpec((1, 1, V7X_LANES), lambda b, g, ch: (g, 0, 0)),
                  pl.BlockSpec((1, dv), fixed2),
                  st_a,
                  pl.BlockSpec((c, c), fixed2), pl.BlockSpec((c, c), fixed2),
                  pl.BlockSpec(mask.shape, fixed3)],
        out_specs=[pl.BlockSpec((c, wv_), lambda b, g, ch: (b * n_chunks + ch, g)), st_a],
        out_shape=[jax.ShapeDtypeStruct((bsz * seq, h_a * dv), BF16), jax.ShapeDtypeStruct(s_g.shape, F32)],
        scratch_shapes=[pltpu.VMEM((hb, dk, dv), F32), pltpu.VMEM((8 + c, wk_), F32),
                        pltpu.VMEM((8 + c, wk_), F32), pltpu.VMEM((8 + c, wv_), F32)],
        compiler_params=_params("parallel", "parallel", "arbitrary"),
        name="gdn_mixer",
    )(proj, proj, proj, proj, proj, conv_g, conv_g, conv_g, conv_gdn_w, conv_gdn_w, conv_gdn_w,
      alog_a, dtb_a, gdn_norm_w.reshape(1, dv), s_g.astype(F32), tri, su, mask)

    h_b, p, n_b, d_inner = lay['h_b'], lay['p_b'], lay['n_b'], lay['d_inner']
    n_r = h_b // G_B
    wx = n_r * p
    xoff = lay['xbc_off'] // wx
    boff = (lay['xbc_off'] + d_inner) // n_b
    coff = (lay['xbc_off'] + d_inner + G_B * n_b) // n_b
    cxo, cbo, cco = 0, d_inner // n_b, (d_inner + G_B * n_b) // n_b
    st_b = pl.BlockSpec((1, n_r, n_b, p), lambda b, g, ch: (b, g, 0, 0))
    alog_b, dtb_b = ssd_rows
    bias2 = conv_ssd_b.reshape(1, -1)
    y_b, s_s_new = pl.pallas_call(
        functools.partial(_ssd_body, c=c, n_r=n_r, p=p),
        grid=(bsz, G_B, n_chunks),
        in_specs=[col(wx, xoff), col(n_b, boff), col(n_b, coff),
                  col(wx, lay['zb_off'] // wx),
                  col(V7X_LANES, lay['small_off'] // V7X_LANES + n_hg),
                  cache(wx, cxo), cache(n_b, cbo), cache(n_b, cco),
                  cw(wx, cxo), cw(n_b, cbo), cw(n_b, cco),
                  pl.BlockSpec((1, wx), lambda b, g, ch: (0, cxo + g)),
                  pl.BlockSpec((1, n_b), lambda b, g, ch: (0, cbo + g)),
                  pl.BlockSpec((1, n_b), lambda b, g, ch: (0, cco + g)),
                  pl.BlockSpec((1, 1, V7X_LANES), lambda b, g, ch: (g, 0, 0)),
                  pl.BlockSpec((1, 1, V7X_LANES), lambda b, g, ch: (g, 0, 0)),
                  pl.BlockSpec((1, wx), lambda b, g, ch: (0, g)),
                  pl.BlockSpec((1, wx), lambda b, g, ch: (0, g)),
                  pl.BlockSpec((V7X_LANES, wx), fixed2),
                  st_b,
                  pl.BlockSpec((c, c), fixed2), pl.BlockSpec((c, c), fixed2)],
        out_specs=[pl.BlockSpec((c, wx), lambda b, g, ch: (b * n_chunks + ch, g)), st_b],
        out_shape=[jax.ShapeDtypeStruct((bsz * seq, d_inner), BF16), jax.ShapeDtypeStruct(s_s.shape, F32)],
        scratch_shapes=[pltpu.VMEM((n_b, wx), F32), pltpu.VMEM((8 + c, wx), F32),
                        pltpu.VMEM((8 + c, n_b), F32), pltpu.VMEM((8 + c, n_b), F32)],
        compiler_params=_params("parallel", "parallel", "arbitrary"),
        name="ssd_mixer",
    )(proj, proj, proj, proj, proj, conv_s, conv_s, conv_s, conv_ssd_w, conv_ssd_w, conv_ssd_w,
      bias2, bias2, bias2, alog_b, dtb_b, ssd_d_row, ssd_norm_w.reshape(1, d_inner), expand,
      s_s.astype(F32), tri, su)
    return o_a, s_g_new, y_b, s_s_new


def _split_cols(t, sizes):
    idx = [int(i) for i in np.cumsum(sizes)[:-1]]
    return jnp.split(t, idx, axis=-1)


def kernel(x_prompt, x_sample, cache_conv_gdn, state_gdn, cache_conv_ssd, state_ssd, state_hgrn, w_in_ab, conv_gdn_w, gdn_a_log, gdn_dt_bias, gdn_norm_w, conv_ssd_w, conv_ssd_b, ssd_a_log, ssd_dt_bias, ssd_d, ssd_norm_w, w_out_ab, w_in_c, hgrn_lower_bounds, hgrn_norm_w, w_out_c, router_w, router_b, expert_w_gate_up, expert_b_gate_up, expert_w_down, expert_b_down, ln_mix_g, ln_mix_b, ln_ffn_g, ln_ffn_b):
    bp, lp, d = x_prompt.shape
    bs, ls, _ = x_sample.shape
    depth = ln_mix_g.shape[0]
    n_pairs = depth // 2
    alpha = float((2 * depth) ** 0.25)
    tp, ts = bp * lp, bs * ls
    h_a, dv_a = gdn_a_log.shape[1], gdn_norm_w.shape[1]
    qkv_a = conv_gdn_w.shape[2]
    dk_a = (qkv_a // h_a - dv_a) // 2
    h_b, d_inner = ssd_a_log.shape[1], ssd_norm_w.shape[1]
    p_b = d_inner // h_b
    xbc_b = conv_ssd_w.shape[2]
    n_b = (xbc_b - d_inner) // (2 * G_B)
    hd_c = hgrn_norm_w.shape[1]
    h_c = d // hd_c
    exp_c = hgrn_lower_bounds.shape[1] // h_c
    assert min(lp, ls) >= CONV_W - 1

    hb_a = _pick(h_a, (8, 4, 2, 1))
    n_hg_a = h_a // hb_a
    r_b = h_b // G_B
    lay = dict(h_a=h_a, dk_a=dk_a, dv_a=dv_a, hb_a=hb_a, h_b=h_b, p_b=p_b, n_b=n_b, d_inner=d_inner,
               za_off=qkv_a, zb_off=qkv_a + h_a * dv_a, xbc_off=qkv_a + h_a * dv_a + d_inner,
               small_off=qkv_a + h_a * dv_a + d_inner + xbc_b)
    assert lay['za_off'] % (hb_a * dv_a) == 0 and lay['zb_off'] % (r_b * p_b) == 0
    assert lay['xbc_off'] % (r_b * p_b) == 0 and (lay['xbc_off'] + d_inner) % n_b == 0
    assert lay['small_off'] % V7X_LANES == 0 and d_inner % n_b == 0
    w_qkv, w_b, w_a, w_za, w_zb, w_xbc, w_dt = _split_cols(
        w_in_ab, [qkv_a, h_a, h_a, h_a * dv_a, d_inner, xbc_b, h_b])
    lane_pad = lambda n: jnp.zeros(w_in_ab.shape[:2] + (n,), w_in_ab.dtype)
    small_cols = []
    for gi in range(n_hg_a):
        hs = slice(gi * hb_a, (gi + 1) * hb_a)
        small_cols += [w_b[..., hs], w_a[..., hs], lane_pad(V7X_LANES - 2 * hb_a)]
    for g in range(G_B):
        small_cols += [w_dt[..., g * r_b:(g + 1) * r_b], lane_pad(V7X_LANES - r_b)]
    n_small = (n_hg_a + G_B) * V7X_LANES
    small_cols.append(lane_pad(-n_small % (2 * V7X_MXU_DIM)))
    w_in_ab_r = jnp.concatenate([w_qkv, w_za, w_zb, w_xbc] + small_cols, axis=-1).astype(BF16)
    on_lanes = lambda t, lo: jnp.pad(t.astype(F32), ((0, 0), (0, 0), (lo, V7X_LANES - lo - t.shape[-1])))[:, :, None, :]
    gdn_rows = (on_lanes(gdn_a_log.reshape(n_pairs, n_hg_a, hb_a), hb_a),
                on_lanes(gdn_dt_bias.reshape(n_pairs, n_hg_a, hb_a), hb_a))
    ssd_rows = (on_lanes(ssd_a_log.reshape(n_pairs, G_B, r_b), 0),
                on_lanes(ssd_dt_bias.reshape(n_pairs, G_B, r_b), 0))
    ssd_d_rows = jnp.repeat(ssd_d.astype(F32), p_b, axis=-1)[:, None, :]
    expand_np = np.zeros((V7X_LANES, r_b * p_b), np.float32)
    for r in range(r_b):
        expand_np[r, r * p_b:(r + 1) * p_b] = 1.0
    expand = jnp.asarray(expand_np, BF16)
    w_in_c_b = w_in_c.astype(BF16)
    w_out_ab_b = w_out_ab.astype(BF16)
    w_out_c_b = w_out_c.astype(BF16)
    w_gu_b = expert_w_gate_up.astype(BF16)
    w_d_b = expert_w_down.astype(BF16)

    lb_w = jax.nn.softmax(hgrn_lower_bounds.astype(F32), axis=0)
    lower_bounds = jnp.cumsum(lb_w, axis=0) - lb_w[0]

    x = jnp.concatenate([x_prompt.reshape(tp, d), x_sample.reshape(ts, d)], axis=0)
    xb = x.astype(BF16)
    zeros = lambda *shape: jnp.zeros(shape, F32)
    new = {key: [] for key in ('cg_p', 'sg_p', 'cs_p', 'ss_p', 'sh_p', 'cg_s', 'sg_s', 'cs_s', 'ss_s', 'sh_s')}
    for layer in range(depth):
        j = layer // 2
        if layer % 2 == 0:
            proj = _matmul(xb, w_in_ab_r[j])
            wts = (conv_gdn_w[j], (gdn_rows[0][j], gdn_rows[1][j]), gdn_norm_w[j], conv_ssd_w[j], conv_ssd_b[j],
                   (ssd_rows[0][j], ssd_rows[1][j]), ssd_d_rows[j], ssd_norm_w[j], expand)
            oa_p, sg, yb_p, ss = _ab_mixers(
                proj, lay, 0, bp, lp, zeros(bp, CONV_W - 1, qkv_a), zeros(bp, h_a, dk_a, dv_a),
                zeros(bp, CONV_W - 1, xbc_b), zeros(bp, h_b, n_b, p_b), *wts)
            new['sg_p'].append(sg)
            new['ss_p'].append(ss)
            oa_s, sg, yb_s, ss = _ab_mixers(
                proj, lay, tp, bs, ls, cache_conv_gdn[j], state_gdn[j], cache_conv_ssd[j], state_ssd[j], *wts)
            new['sg_s'].append(sg)
            new['ss_s'].append(ss)
            nk = CONV_W - 1
            tail_p = jnp.stack([proj[(b + 1) * lp - nk:(b + 1) * lp] for b in range(bp)])
            tail_s = jnp.stack([proj[tp + (b + 1) * ls - nk:tp + (b + 1) * ls] for b in range(bs)])
            xbc_cols = slice(lay['xbc_off'], lay['xbc_off'] + xbc_b)
            new['cg_p'].append(tail_p[..., :qkv_a])
            new['cs_p'].append(tail_p[..., xbc_cols])
            new['cg_s'].append(tail_s[..., :qkv_a])
            new['cs_s'].append(tail_s[..., xbc_cols])
            mixes = [jnp.concatenate([oa_p, oa_s], axis=0), jnp.concatenate([yb_p, yb_s], axis=0)]
            ws = [w_out_ab_b[j, :h_a * dv_a], w_out_ab_b[j, h_a * dv_a:]]
        else:
            proj = _matmul(xb, w_in_c_b[j])
            mix_p, sh = _gla_mixer(proj, 0, bp, lp, lower_bounds[j], zeros(bp, h_c, exp_c, hd_c), hgrn_norm_w[j])
            new['sh_p'].append(sh)
            mix_s, sh = _gla_mixer(proj, tp, bs, ls, lower_bounds[j], state_hgrn[j], hgrn_norm_w[j])
            new['sh_s'].append(sh)
            mixes = [jnp.concatenate([mix_p, mix_s], axis=0)]
            ws = [w_out_c_b[j]]
        x, xp = _outproj_ln(mixes, ws, x, ln_mix_g[layer], ln_mix_b[layer], alpha)
        x, xb = _moe_ln(x, xp, layer, router_w[layer], router_b[layer], w_gu_b, expert_b_gate_up,
                        w_d_b, expert_b_down, ln_ffn_g[layer], ln_ffn_b[layer], alpha)
    st = {key: jnp.stack(val) for key, val in new.items()}
    return (x[:tp].reshape(bp, lp, d), x[tp:].reshape(bs, ls, d),
            st['cg_p'], st['sg_p'], st['cs_p'], st['ss_p'], st['sh_p'],
            st['cg_s'], st['sg_s'], st['cs_s'], st['ss_s'], st['sh_s'])
```
